```python
import math
import jax
import jax.numpy as jnp
from jax import lax
import numpy as np

D_MODEL = 1024
BATCH = 1
SEQ = 16384
DEPTH = 2
DEC_BATCH = 32
DEC_SEQ = 8
PAST_LEN = 16384
PAGE_SIZE = 128

CONV_W = D_MODEL // 4
ATT_W = D_MODEL // 2
RWKV_W = D_MODEL - CONV_W - ATT_W
ATT_HEADS = 4
ATT_DH = ATT_W // (2 * ATT_HEADS)
RWKV_DH = 64
RWKV_HEADS = RWKV_W // RWKV_DH
CONV_K = 31
W_LORA = 64
A_LORA = 64
G_LORA = 128
RWKV_IN = 3 * RWKV_W + W_LORA + A_LORA + G_LORA
IN_W = 2 * CONV_W + 3 * ATT_W + RWKV_IN
D_FF = (8 * D_MODEL + 3 * 256 - 1) // (3 * 256) * 256
ROPE_THETA = 10000.0
NORM_EPS = 1e-6
LN_EPS = 1e-5
RWKV_GN_EPS = 64e-5
Q_BLOCK = 128
NEG_INF = -1e30

kernel_name = 'hybrid_conv_diffattn_rwkv7_decode_step'


def rms_norm(x, g):
    xf = x.astype(jnp.float32)
    y = xf * lax.rsqrt(jnp.mean(xf * xf, axis=-1, keepdims=True) + NORM_EPS)
    return (y * g.astype(jnp.float32)).astype(x.dtype)


def layer_norm(x, g, b, eps):
    xf = x.astype(jnp.float32)
    mu = jnp.mean(xf, axis=-1, keepdims=True)
    var = jnp.mean(jnp.square(xf - mu), axis=-1, keepdims=True)
    return (xf - mu) * lax.rsqrt(var + eps) * g.astype(jnp.float32) + b.astype(jnp.float32)


def rope(x, pos):
    half = x.shape[-1] // 2
    inv = ROPE_THETA ** (-jnp.arange(half, dtype=jnp.float32) / half)
    ang = pos.astype(jnp.float32)[:, None] * inv[None, :]
    cos = jnp.cos(ang)[None, :, None, :]
    sin = jnp.sin(ang)[None, :, None, :]
    xf = x.astype(jnp.float32)
    x1, x2 = xf[..., :half], xf[..., half:]
    return jnp.concatenate([x1 * cos - x2 * sin, x2 * cos + x1 * sin], axis=-1).astype(x.dtype)


def conv_module(z, conv_st, conv_w, conv_b, ln_g, ln_b, pw_w, pw_b):
    u = z[..., :CONV_W] * jax.nn.sigmoid(z[..., CONV_W:])
    padded = jnp.concatenate([conv_st.astype(u.dtype), u], axis=1)
    y = lax.conv_general_dilated(padded, conv_w.astype(u.dtype), (1,), 'VALID',
                                 dimension_numbers=('NWC', 'WIO', 'NWC'),
                                 feature_group_count=CONV_W) + conv_b
    y = layer_norm(y, ln_g, ln_b, LN_EPS).astype(u.dtype)
    y = jax.nn.silu(y) @ pw_w + pw_b
    return y, padded[:, -(CONV_K - 1):]


def diff_attention(q, k, v, lam, past_k, past_v):
    scale = ATT_DH ** -0.5
    B, T = q.shape[0], q.shape[1]
    if past_k is None:
        nb = T // Q_BLOCK
        qb = jnp.moveaxis(q.reshape(B, nb, Q_BLOCK, ATT_HEADS, 2, ATT_DH), 1, 0)
        kpos = jnp.arange(T)

        def block(args):
            qi, i = args
            qpos = i * Q_BLOCK + jnp.arange(Q_BLOCK)
            logits = jnp.einsum('bqhcd,bkhcd->bhcqk', qi, k,
                                preferred_element_type=jnp.float32) * scale
            logits = jnp.where(kpos[None, :] <= qpos[:, None], logits, NEG_INF)
            p = jax.nn.softmax(logits, axis=-1)
            pd = (p[:, :, 0] - lam * p[:, :, 1]).astype(v.dtype)
            return jnp.einsum('bhqk,bkhe->bqhe', pd, v)

        out = lax.map(block, (qb, jnp.arange(nb)))
        return jnp.moveaxis(out, 0, 1).reshape(B, T, ATT_HEADS, 2 * ATT_DH)
    P = past_k.shape[1]
    lp = jnp.einsum('bqhcd,bkhcd->bhcqk', q, past_k, preferred_element_type=jnp.float32) * scale
    ln = jnp.einsum('bqhcd,bkhcd->bhcqk', q, k, preferred_element_type=jnp.float32) * scale
    causal = jnp.arange(T)[None, :] <= jnp.arange(T)[:, None]
    ln = jnp.where(causal, ln, NEG_INF)
    p = jax.nn.softmax(jnp.concatenate([lp, ln], axis=-1), axis=-1)
    pd = (p[:, :, 0] - lam * p[:, :, 1]).astype(v.dtype)
    return (jnp.einsum('bhqk,bkhe->bqhe', pd[..., :P], past_v)
            + jnp.einsum('bhqk,bkhe->bqhe', pd[..., P:], v))


def rwkv7_mixer(z, shift_st, wkv_st, mu, w0, w2, a0, a2, g2, k_k, k_a, r_k, ln_g, ln_b):
    B, T = z.shape[0], z.shape[1]
    prev = jnp.concatenate([shift_st[:, None].astype(z.dtype), z[:, :-1]], axis=1)
    zs = z + (prev - z) * mu
    r, k, v, wd, ad, gd = jnp.split(
        zs, [RWKV_W, 2 * RWKV_W, 3 * RWKV_W, 3 * RWKV_W + W_LORA, 3 * RWKV_W + W_LORA + A_LORA], axis=-1)
    w_log = -jax.nn.softplus(-(w0 + jnp.tanh(wd) @ w2)) - 0.5
    decay = jnp.exp(-jnp.exp(w_log.astype(jnp.float32)))
    a = jax.nn.sigmoid(a0 + ad @ a2)
    g = jax.nn.sigmoid(gd) @ g2

    def heads(t):
        return t.reshape(B, T, RWKV_HEADS, RWKV_DH).astype(jnp.float32)

    kk = heads(k * k_k)
    kk = kk * lax.rsqrt(jnp.sum(kk * kk, axis=-1, keepdims=True) + 1e-12)
    k = k * (1.0 + (a - 1.0) * k_a)
    r_h, k_h, v_h, a_h, w_h = heads(r), heads(k), heads(v), heads(a), heads(decay)

    def step(S, inp):
        r_t, w_t, k_t, v_t, kk_t, b_t = inp
        sa = jnp.einsum('bhvk,bhk->bhv', S, kk_t)
        S = S * w_t[:, :, None, :] - sa[..., None] * b_t[:, :, None, :] + v_t[..., None] * k_t[:, :, None, :]
        return S, jnp.einsum('bhvk,bhk->bhv', S, r_t)

    tm = lambda t: jnp.moveaxis(t, 1, 0)
    S_fin, y = lax.scan(step, wkv_st.astype(jnp.float32),
                        (tm(r_h), tm(w_h), tm(k_h), tm(v_h), tm(kk), tm(kk * a_h)))
    y = jnp.moveaxis(y, 0, 1)
    mean = jnp.mean(y, axis=-1, keepdims=True)
    var = jnp.mean(jnp.square(y - mean), axis=-1, keepdims=True)
    y = (y - mean) * lax.rsqrt(var + RWKV_GN_EPS)
    y = y * ln_g.astype(jnp.float32).reshape(RWKV_HEADS, RWKV_DH) + ln_b.astype(jnp.float32).reshape(RWKV_HEADS, RWKV_DH)
    y = y + jnp.sum(r_h * k_h * r_k.astype(jnp.float32), axis=-1, keepdims=True) * v_h
    out = y.reshape(B, T, RWKV_W).astype(z.dtype) * g
    return out, z[:, -1], S_fin.astype(wkv_st.dtype)


def _layer(x, pos, lam_init, conv_st, shift_st, wkv_st, past_k, past_v,
           norm1_g, w_in, conv_w, conv_b, conv_ln_g, conv_ln_b, conv_pw_w, conv_pw_b,
           att_lambda, att_subln_g, rwkv_mu, rwkv_w0, rwkv_w2, rwkv_a0, rwkv_a2, rwkv_g2,
           rwkv_kk, rwkv_ka, rwkv_rk, rwkv_ln_g, rwkv_ln_b, group_scale, w_out,
           norm2_g, w_gate, w_up, w_down):
    B, T = x.shape[0], x.shape[1]
    h = rms_norm(x, norm1_g)
    z = h @ w_in
    z_conv, z_att, z_rwkv = jnp.split(z, [2 * CONV_W, 2 * CONV_W + 3 * ATT_W], axis=-1)
    a_out, conv_new = conv_module(z_conv, conv_st, conv_w, conv_b, conv_ln_g, conv_ln_b, conv_pw_w, conv_pw_b)
    q, k, v = jnp.split(z_att, 3, axis=-1)
    q = rope(q.reshape(B, T, 2 * ATT_HEADS, ATT_DH), pos).reshape(B, T, ATT_HEADS, 2, ATT_DH)
    k = rope(k.reshape(B, T, 2 * ATT_HEADS, ATT_DH), pos).reshape(B, T, ATT_HEADS, 2, ATT_DH)
    v = v.reshape(B, T, ATT_HEADS, 2 * ATT_DH)
    lam_p = att_lambda.astype(jnp.float32)
    lam = jnp.exp(jnp.sum(lam_p[0] * lam_p[1])) - jnp.exp(jnp.sum(lam_p[2] * lam_p[3])) + lam_init
    o = diff_attention(q, k, v, lam, past_k, past_v)
    b_out = (rms_norm(o, att_subln_g) * (1.0 - lam_init)).reshape(B, T, ATT_W)
    c_out, shift_new, wkv_new = rwkv7_mixer(z_rwkv, shift_st, wkv_st, rwkv_mu, rwkv_w0, rwkv_w2, rwkv_a0,
                                            rwkv_a2, rwkv_g2, rwkv_kk, rwkv_ka, rwkv_rk, rwkv_ln_g, rwkv_ln_b)
    mix = jnp.concatenate([a_out, b_out, c_out], axis=-1) * group_scale
    x = x + mix @ w_out
    h2 = rms_norm(x, norm2_g)
    x = x + (jax.nn.silu(h2 @ w_gate) * (h2 @ w_up)) @ w_down
    return x, k.reshape(B, T, ATT_HEADS, 2 * ATT_DH), v, conv_new, shift_new, wkv_new


def setup_inputs(seed: int = 0) -> dict:
    key = jax.random.key(seed)
    ks = jax.random.split(key, 40)
    f32 = jnp.float32

    def nrm(i, shape, s):
        return s * jax.random.normal(ks[i], shape, f32)

    n_pages = PAST_LEN // PAGE_SIZE
    n_used = DEC_BATCH * n_pages
    n_pool = n_used + max(1, n_used // 4)
    page_table = jax.random.permutation(ks[0], n_pool)[:n_used].reshape(DEC_BATCH, n_pages).astype(jnp.int32)
    return {
        'x_prompt': nrm(1, (BATCH, SEQ, D_MODEL), 1.0),
        'x_sample': nrm(2, (DEC_BATCH, DEC_SEQ, D_MODEL), 1.0),
        'cache_k': nrm(3, (DEPTH, n_pool, PAGE_SIZE, ATT_HEADS, 2 * ATT_DH), 1.0),
        'cache_v': nrm(4, (DEPTH, n_pool, PAGE_SIZE, ATT_HEADS, 2 * ATT_DH), 1.0),
        'state_conv': nrm(5, (DEPTH, DEC_BATCH, CONV_K - 1, CONV_W), 0.5),
        'state_shift': nrm(6, (DEPTH, DEC_BATCH, RWKV_IN), 1.0),
        'state_wkv': nrm(7, (DEPTH, DEC_BATCH, RWKV_HEADS, RWKV_DH, RWKV_DH), 0.5),
        'page_table': page_table,
        'norm1_g': 1.0 + nrm(8, (DEPTH, D_MODEL), 0.02),
        'w_in': nrm(9, (DEPTH, D_MODEL, IN_W), D_MODEL ** -0.5),
        'conv_w': nrm(10, (DEPTH, CONV_K, 1, CONV_W), CONV_K ** -0.5),
        'conv_b': nrm(11, (DEPTH, CONV_W), 0.02),
        'conv_ln_g': 1.0 + nrm(12, (DEPTH, CONV_W), 0.02),
        'conv_ln_b': nrm(13, (DEPTH, CONV_W), 0.02),
        'conv_pw_w': nrm(14, (DEPTH, CONV_W, CONV_W), CONV_W ** -0.5),
        'conv_pw_b': nrm(15, (DEPTH, CONV_W), 0.02),
        'att_lambda': nrm(16, (DEPTH, 4, ATT_DH), 0.1),
        'att_subln_g': 1.0 + nrm(17, (DEPTH, 2 * ATT_DH), 0.02),
        'rwkv_mu': jax.random.uniform(ks[18], (DEPTH, RWKV_IN), f32, 0.0, 1.0),
        'rwkv_w0': jax.random.uniform(ks[19], (DEPTH, RWKV_W), f32, -6.0, 0.0),
        'rwkv_w2': nrm(20, (DEPTH, W_LORA, RWKV_W), 0.1),
        'rwkv_a0': nrm(21, (DEPTH, RWKV_W), 0.1),
        'rwkv_a2': nrm(22, (DEPTH, A_LORA, RWKV_W), A_LORA ** -0.5),
        'rwkv_g2': nrm(23, (DEPTH, G_LORA, RWKV_W), G_LORA ** -0.5),
        'rwkv_kk': 0.85 + nrm(24, (DEPTH, RWKV_W), 0.02),
        'rwkv_ka': 1.0 + nrm(25, (DEPTH, RWKV_W), 0.02),
        'rwkv_rk': nrm(26, (DEPTH, RWKV_HEADS, RWKV_DH), 0.1),
        'rwkv_ln_g': 1.0 + nrm(27, (DEPTH, RWKV_W), 0.02),
        'rwkv_ln_b': nrm(28, (DEPTH, RWKV_W), 0.02),
        'group_scale': 1.0 + nrm(29, (DEPTH, D_MODEL), 0.02),
        'w_out': nrm(30, (DEPTH, D_MODEL, D_MODEL), D_MODEL ** -0.5),
        'norm2_g': 1.0 + nrm(31, (DEPTH, D_MODEL), 0.02),
        'w_gate': nrm(32, (DEPTH, D_MODEL, D_FF), D_MODEL ** -0.5),
        'w_up': nrm(33, (DEPTH, D_MODEL, D_FF), D_MODEL ** -0.5),
        'w_down': nrm(34, (DEPTH, D_FF, D_MODEL), D_FF ** -0.5),
        'final_g': 1.0 + nrm(35, (D_MODEL,), 0.02),
    }


def reference(x_prompt, x_sample, cache_k, cache_v, state_conv, state_shift, state_wkv, page_table,
              norm1_g, w_in, conv_w, conv_b, conv_ln_g, conv_ln_b, conv_pw_w, conv_pw_b,
              att_lambda, att_subln_g, rwkv_mu, rwkv_w0, rwkv_w2, rwkv_a0, rwkv_a2, rwkv_g2,
              rwkv_kk, rwkv_ka, rwkv_rk, rwkv_ln_g, rwkv_ln_b, group_scale, w_out,
              norm2_g, w_gate, w_up, w_down, final_g):
    Bp, Tp = x_prompt.shape[0], x_prompt.shape[1]
    Bs, Ts = x_sample.shape[0], x_sample.shape[1]
    n_pages = page_table.shape[1]
    past_len = n_pages * PAGE_SIZE
    pos_p = jnp.arange(Tp, dtype=jnp.int32)
    pos_s = past_len + jnp.arange(Ts, dtype=jnp.int32)
    dt = x_prompt.dtype
    xp, xs = x_prompt, x_sample
    kp, vp, cp, sp, wp = [], [], [], [], []
    ksm, vsm, csm, ssm, wsm = [], [], [], [], []
    for l in range(DEPTH):
        lam_init = 0.8 - 0.6 * math.exp(-0.3 * l)
        wl = (norm1_g[l], w_in[l], conv_w[l], conv_b[l], conv_ln_g[l], conv_ln_b[l], conv_pw_w[l], conv_pw_b[l],
              att_lambda[l], att_subln_g[l], rwkv_mu[l], rwkv_w0[l], rwkv_w2[l], rwkv_a0[l], rwkv_a2[l],
              rwkv_g2[l], rwkv_kk[l], rwkv_ka[l], rwkv_rk[l], rwkv_ln_g[l], rwkv_ln_b[l], group_scale[l],
              w_out[l], norm2_g[l], w_gate[l], w_up[l], w_down[l])
        xp, k_, v_, c_, s_, w_ = _layer(
            xp, pos_p, lam_init,
            jnp.zeros((Bp, CONV_K - 1, CONV_W), dt), jnp.zeros((Bp, RWKV_IN), dt),
            jnp.zeros((Bp, RWKV_HEADS, RWKV_DH, RWKV_DH), dt), None, None, *wl)
        kp.append(k_); vp.append(v_); cp.append(c_); sp.append(s_); wp.append(w_)
        past_k = cache_k[l][page_table].reshape(Bs, past_len, ATT_HEADS, 2, ATT_DH)
        past_v = cache_v[l][page_table].reshape(Bs, past_len, ATT_HEADS, 2 * ATT_DH)
        xs, k_, v_, c_, s_, w_ = _layer(
            xs, pos_s, lam_init, state_conv[l], state_shift[l], state_wkv[l], past_k, past_v, *wl)
        ksm.append(k_); vsm.append(v_); csm.append(c_); ssm.append(s_); wsm.append(w_)
    y_prompt = rms_norm(xp, final_g)
    y_sample = rms_norm(xs, final_g)
    return (y_prompt, y_sample,
            jnp.stack(kp), jnp.stack(vp), jnp.stack(cp), jnp.stack(sp), jnp.stack(wp),
            jnp.stack(ksm), jnp.stack(vsm), jnp.stack(csm), jnp.stack(ssm), jnp.stack(wsm))
```

```python
import functools
import math

import jax
import jax.numpy as jnp
import numpy as np
from jax import lax
from jax.experimental import pallas as pl
from jax.experimental.pallas import tpu as pltpu

D_MODEL = 1024
DEPTH = 2
PAGE_SIZE = 128
CONV_W = 256
ATT_W = 512
RWKV_W = 256
ATT_HEADS = 4
ATT_DH = 64
HEAD_W = 2 * ATT_DH
RWKV_DH = 64
RWKV_HEADS = 4
CONV_K = 31
W_LORA = 64
A_LORA = 64
G_LORA = 128
RWKV_IN = 3 * RWKV_W + W_LORA + A_LORA + G_LORA
IN_W = 2 * CONV_W + 3 * ATT_W + RWKV_IN
D_FF = 2816
ROPE_THETA = 10000.0
NORM_EPS = 1e-6
LN_EPS = 1e-5
RWKV_GN_EPS = 64e-5
NEG_INF = -1e30

F32 = jnp.float32
BF16 = jnp.bfloat16
HI = lax.Precision.HIGHEST

CONV_HALO = 32
CONV_ROWS = 64
FF_CHUNK = 256
SCAN_TILE = 128
ATT_BLOCK = 1024
ATT_KCHUNK = 512
PAGES_PER_STEP = 8
VMEM_LIMIT = 56 * 1024 * 1024


def _const_spec(shape):
    zeros = (0,) * len(shape)
    return pl.BlockSpec(shape, lambda *_: zeros, pipeline_mode=pl.Buffered(1))


def _params(sem):
    return pltpu.CompilerParams(dimension_semantics=sem, vmem_limit_bytes=VMEM_LIMIT)


def _rms(x, g):
    return x * lax.rsqrt(jnp.mean(x * x, axis=-1, keepdims=True) + NORM_EPS) * g


def _in_proj_kernel(x_ref, g_ref, w_ref, cos_ref, sin_ref, *out_refs, emit_bf16):
    if emit_bf16:
        zc_ref, q_ref, k_ref, v_ref, zr_ref, kb_ref, vb_ref = out_refs
    else:
        zc_ref, q_ref, k_ref, v_ref, zr_ref = out_refs
    h = _rms(x_ref[...], g_ref[...]).astype(BF16)

    def mm(c0, c1):
        return jnp.dot(h, w_ref[:, c0:c1], preferred_element_type=F32)

    cos = jnp.concatenate([cos_ref[...]] * ATT_HEADS, axis=-1)
    sin = jnp.concatenate([sin_ref[...]] * ATT_HEADS, axis=-1)
    q0, k0, v0, r0 = 2 * CONV_W, 2 * CONV_W + ATT_W, 2 * CONV_W + 2 * ATT_W, 2 * CONV_W + 3 * ATT_W
    zc_ref[...] = mm(0, q0)
    q = mm(q0, k0) * cos + mm(IN_W, IN_W + ATT_W) * sin
    q_ref[...] = q * (ATT_DH ** -0.5)
    k = mm(k0, v0) * cos + mm(IN_W + ATT_W, IN_W + 2 * ATT_W) * sin
    k_ref[...] = k
    v = mm(v0, r0)
    v_ref[...] = v
    zr_ref[...] = mm(r0, IN_W)
    if emit_bf16:
        kb_ref[...] = k.astype(BF16)
        vb_ref[...] = v.astype(BF16)


def _in_proj(x, g, w_ext, cos, sin, *, tm, emit_bf16):
    n = x.shape[0]
    row = lambda w: pl.BlockSpec((tm, w), lambda i: (i, 0))
    widths = [2 * CONV_W, ATT_W, ATT_W, ATT_W, RWKV_IN]
    out_shape = [jax.ShapeDtypeStruct((n, w), F32) for w in widths]
    out_specs = [row(w) for w in widths]
    if emit_bf16:
        out_shape += [jax.ShapeDtypeStruct((n, ATT_W), BF16)] * 2
        out_specs += [row(ATT_W)] * 2
    return pl.pallas_call(
        functools.partial(_in_proj_kernel, emit_bf16=emit_bf16),
        grid=(n // tm,),
        in_specs=[row(D_MODEL), _const_spec((1, D_MODEL)), _const_spec(w_ext.shape),
                  row(HEAD_W), row(HEAD_W)],
        out_specs=out_specs,
        out_shape=out_shape,
        compiler_params=_params(("parallel",)),
        name="in_proj",
    )(x, g, w_ext, cos, sin)


def _conv_kernel(zc_ref, st_ref, cw_ref, cb_ref, lg_ref, lb_ref, pw_ref, pb_ref,
                 a_ref, cn_ref, pad_ref, *, tt):
    @pl.when(pl.program_id(1) == 0)
    def _():
        pad_ref[0:CONV_HALO, :] = st_ref[0]

    z = zc_ref[0]
    pad_ref[CONV_HALO:CONV_HALO + tt, :] = z[:, :CONV_W] * jax.nn.sigmoid(z[:, CONV_W:])
    rc = min(tt, CONV_ROWS)
    first = CONV_HALO - (CONV_K - 1)
    for r0 in range(0, tt, rc):
        acc = jnp.zeros((rc, CONV_W), F32)
        for j in range(CONV_K):
            acc = acc + cw_ref[j:j + 1, :] * pad_ref[r0 + first + j:r0 + first + j + rc, :]
        y = acc + cb_ref[...]
        mu = jnp.mean(y, axis=-1, keepdims=True)
        d = y - mu
        var = jnp.mean(d * d, axis=-1, keepdims=True)
        y = d * lax.rsqrt(var + LN_EPS) * lg_ref[...] + lb_ref[...]
        y = (y * jax.nn.sigmoid(y)).astype(BF16)
        a_ref[0, r0:r0 + rc, :] = jnp.dot(y, pw_ref[...], preferred_element_type=F32) + pb_ref[...]
    tail = pad_ref[tt:tt + CONV_HALO, :]
    cn_ref[0] = tail
    pad_ref[0:CONV_HALO, :] = tail


def _conv(zc, state, cw, cb, lg, lb, pw, pb, *, tt):
    b, t, _ = zc.shape
    return pl.pallas_call(
        functools.partial(_conv_kernel, tt=tt),
        grid=(b, t // tt),
        in_specs=[pl.BlockSpec((1, tt, 2 * CONV_W), lambda i, j: (i, j, 0)),
                  pl.BlockSpec((1, CONV_HALO, CONV_W), lambda i, j: (i, 0, 0)),
                  _const_spec((CONV_K, CONV_W)), _const_spec((1, CONV_W)), _const_spec((1, CONV_W)),
                  _const_spec((1, CONV_W)), _const_spec((CONV_W, CONV_W)), _const_spec((1, CONV_W))],
        out_specs=[pl.BlockSpec((1, tt, CONV_W), lambda i, j: (i, j, 0)),
                   pl.BlockSpec((1, CONV_HALO, CONV_W), lambda i, j: (i, 0, 0))],
        out_shape=[jax.ShapeDtypeStruct((b, t, CONV_W), F32),
                   jax.ShapeDtypeStruct((b, CONV_HALO, CONV_W), F32)],
        scratch_shapes=[pltpu.VMEM((CONV_HALO + tt, CONV_W), F32)],
        compiler_params=_params(("parallel", "arbitrary")),
        name="conv_module",
    )(zc, state, cw, cb, lg, lb, pw, pb)


def _lam(lam_ref, lam_init):
    lp = lam_ref[...]
    s1 = jnp.sum(lp[0:1] * lp[1:2], axis=-1, keepdims=True)
    s2 = jnp.sum(lp[2:3] * lp[3:4], axis=-1, keepdims=True)
    return jnp.exp(s1) - jnp.exp(s2) + lam_init


def _comp_masks(shape):
    lane = lax.broadcasted_iota(jnp.int32, shape, len(shape) - 1)
    return lane < ATT_DH


def _softmax_update(s, v, m_ref, l_ref, acc_ref):
    m_old = m_ref[...]
    m_new = jnp.maximum(m_old, jnp.max(s, axis=-1, keepdims=True))
    alpha = jnp.exp(m_old - m_new)
    p = jnp.exp(s - m_new)
    l_ref[...] = alpha * l_ref[...] + jnp.sum(p, axis=-1, keepdims=True)
    acc_ref[...] = alpha * acc_ref[...] + jnp.dot(p.astype(BF16), v, preferred_element_type=F32)
    m_ref[...] = m_new


def _diff_combine(acc0, l0, acc1, l1, lam, g, lam_init):
    o = acc0 / l0 - lam * (acc1 / l1)
    return _rms(o, g) * (1.0 - lam_init)


def _attn_prompt_kernel(qt_ref, kt_ref, q_ref, k_ref, v_ref, lam_ref, g_ref, o_ref,
                        m_ref, l_ref, acc_ref, *, lam_init):
    p_id = pl.program_id(1)
    qi = qt_ref[p_id]
    ki = kt_ref[p_id]

    @pl.when(ki == 0)
    def _():
        m_ref[...] = jnp.full(m_ref.shape, NEG_INF, F32)
        l_ref[...] = jnp.zeros(l_ref.shape, F32)
        acc_ref[...] = jnp.zeros(acc_ref.shape, F32)

    q = q_ref[...]
    first = _comp_masks(q.shape)
    qc = (jnp.where(first, q, 0.0).astype(BF16), jnp.where(first, 0.0, q).astype(BF16))

    def block(masked):
        for j in range(0, ATT_BLOCK, ATT_KCHUNK):
            kj = k_ref[j:j + ATT_KCHUNK, :]
            vj = v_ref[j:j + ATT_KCHUNK, :]
            for c in range(2):
                s = lax.dot_general(qc[c], kj, (((1,), (1,)), ((), ())), preferred_element_type=F32)
                if masked:
                    row = lax.broadcasted_iota(jnp.int32, s.shape, 0)
                    col = lax.broadcasted_iota(jnp.int32, s.shape, 1) + j
                    s = jnp.where(col <= row, s, NEG_INF)
                _softmax_update(s, vj, m_ref.at[c], l_ref.at[c], acc_ref.at[c])

    @pl.when(ki < qi)
    def _():
        block(False)

    @pl.when(ki == qi)
    def _():
        block(True)
        o_ref[...] = _diff_combine(acc_ref[0], l_ref[0], acc_ref[1], l_ref[1],
                                   _lam(lam_ref, lam_init), g_ref[...], lam_init)


def _attn_prompt(q, kb, vb, lam_p, g, *, lam_init):
    n = q.shape[0]
    nb = n // ATT_BLOCK
    pairs = [(i, j) for i in range(nb) for j in range(i + 1)]
    q_tab = jnp.asarray(np.array([p[0] for p in pairs], np.int32))
    k_tab = jnp.asarray(np.array([p[1] for p in pairs], np.int32))
    grid_spec = pltpu.PrefetchScalarGridSpec(
        num_scalar_prefetch=2,
        grid=(ATT_HEADS, len(pairs)),
        in_specs=[pl.BlockSpec((ATT_BLOCK, HEAD_W), lambda h, p, qt, kt: (qt[p], h)),
                  pl.BlockSpec((ATT_BLOCK, HEAD_W), lambda h, p, qt, kt: (kt[p], h)),
                  pl.BlockSpec((ATT_BLOCK, HEAD_W), lambda h, p, qt, kt: (kt[p], h)),
                  pl.BlockSpec((4, ATT_DH), lambda h, p, qt, kt: (0, 0)),
                  pl.BlockSpec((1, HEAD_W), lambda h, p, qt, kt: (0, 0))],
        out_specs=pl.BlockSpec((ATT_BLOCK, HEAD_W), lambda h, p, qt, kt: (qt[p], h)),
        scratch_shapes=[pltpu.VMEM((2, ATT_BLOCK, 1), F32), pltpu.VMEM((2, ATT_BLOCK, 1), F32),
                        pltpu.VMEM((2, ATT_BLOCK, HEAD_W), F32)],
    )
    return pl.pallas_call(
        functools.partial(_attn_prompt_kernel, lam_init=lam_init),
        grid_spec=grid_spec,
        out_shape=jax.ShapeDtypeStruct((n, ATT_W), F32),
        compiler_params=_params(("parallel", "arbitrary")),
        name="attn_prompt",
    )(q_tab, k_tab, q, kb, vb, lam_p, g)


def _attn_sample_kernel(pt_ref, q_ref, kn_ref, vn_ref, *rest, lam_init, t_new):
    del pt_ref
    k_refs = rest[:PAGES_PER_STEP]
    v_refs = rest[PAGES_PER_STEP:2 * PAGES_PER_STEP]
    lam_ref, g_ref, o_ref, m_ref, l_ref, acc_ref = rest[2 * PAGES_PER_STEP:]
    step = pl.program_id(1)
    rows = 2 * t_new

    q = q_ref[...]
    q2 = jnp.concatenate([q, q], axis=0)
    lane = lax.broadcasted_iota(jnp.int32, q2.shape, 1)
    rowi = lax.broadcasted_iota(jnp.int32, q2.shape, 0)
    keep = ((lane % HEAD_W) < ATT_DH) == (rowi < t_new)
    qm = jnp.where(keep, q2, 0.0).astype(BF16)

    def logits(h, keys):
        return lax.dot_general(qm[:, h * HEAD_W:(h + 1) * HEAD_W], keys,
                               (((1,), (1,)), ((), ())), preferred_element_type=F32)

    @pl.when(step == 0)
    def _():
        m_ref[...] = jnp.full(m_ref.shape, NEG_INF, F32)
        l_ref[...] = jnp.zeros(l_ref.shape, F32)
        acc_ref[...] = jnp.zeros(acc_ref.shape, F32)
        zpad = jnp.zeros((PAGE_SIZE - t_new, ATT_W), F32)
        kn = jnp.concatenate([kn_ref[...], zpad], axis=0).astype(BF16)
        vn = jnp.concatenate([vn_ref[...], zpad], axis=0).astype(BF16)
        col = lax.broadcasted_iota(jnp.int32, (rows, PAGE_SIZE), 1)
        tok = lax.broadcasted_iota(jnp.int32, (rows, PAGE_SIZE), 0) % t_new
        valid = col <= tok
        for h in range(ATT_HEADS):
            sl = slice(h * HEAD_W, (h + 1) * HEAD_W)
            s = jnp.where(valid, logits(h, kn[:, sl]), NEG_INF)
            _softmax_update(s, vn[:, sl], m_ref.at[h], l_ref.at[h], acc_ref.at[h])

    for h in range(ATT_HEADS):
        sl = slice(h * HEAD_W, (h + 1) * HEAD_W)
        kh = jnp.concatenate([r[:, sl] for r in k_refs], axis=0).astype(BF16)
        vh = jnp.concatenate([r[:, sl] for r in v_refs], axis=0).astype(BF16)
        _softmax_update(logits(h, kh), vh, m_ref.at[h], l_ref.at[h], acc_ref.at[h])

    @pl.when(step == pl.num_programs(1) - 1)
    def _():
        lam = _lam(lam_ref, lam_init)
        outs = []
        for h in range(ATT_HEADS):
            acc = acc_ref[h]
            l = l_ref[h]
            outs.append(_diff_combine(acc[:t_new], l[:t_new], acc[t_new:], l[t_new:],
                                      lam, g_ref[...], lam_init))
        o_ref[...] = jnp.concatenate(outs, axis=-1)


def _attn_sample(q, k_new, v_new, cache_k, cache_v, page_table, lam_p, g, *, layer, lam_init, t_new):
    n = q.shape[0]
    b = n // t_new
    n_pages = page_table.shape[1]
    steps = n_pages // PAGES_PER_STEP
    pool = cache_k.shape[1]
    ck = cache_k.reshape(DEPTH, pool, PAGE_SIZE, ATT_W)
    cv = cache_v.reshape(DEPTH, pool, PAGE_SIZE, ATT_W)
    row = pl.BlockSpec((t_new, ATT_W), lambda i, s, pt: (i, 0))

    def page_spec(j):
        return pl.BlockSpec((None, None, PAGE_SIZE, ATT_W),
                            lambda i, s, pt: (layer, pt[i, s * PAGES_PER_STEP + j], 0, 0))

    pages = [page_spec(j) for j in range(PAGES_PER_STEP)]
    rows = 2 * t_new
    grid_spec = pltpu.PrefetchScalarGridSpec(
        num_scalar_prefetch=1,
        grid=(b, steps),
        in_specs=[row, row, row] + pages + pages + [
            pl.BlockSpec((4, ATT_DH), lambda i, s, pt: (0, 0)),
            pl.BlockSpec((1, HEAD_W), lambda i, s, pt: (0, 0))],
        out_specs=row,
        scratch_shapes=[pltpu.VMEM((ATT_HEADS, rows, 1), F32), pltpu.VMEM((ATT_HEADS, rows, 1), F32),
                        pltpu.VMEM((ATT_HEADS, rows, HEAD_W), F32)],
    )
    return pl.pallas_call(
        functools.partial(_attn_sample_kernel, lam_init=lam_init, t_new=t_new),
        grid_spec=grid_spec,
        out_shape=jax.ShapeDtypeStruct((n, ATT_W), F32),
        compiler_params=_params(("parallel", "arbitrary")),
        name="attn_sample",
    )(page_table, q, k_new, v_new, *([ck] * PAGES_PER_STEP), *([cv] * PAGES_PER_STEP), lam_p, g)


def _rwkv_kernel(z_ref, sh_ref, s0_ref, mu_ref, w0_ref, w2_ref, a0_ref, a2_ref, g2_ref,
                 kk_ref, ka_ref, rk_ref, lng_ref, lnb_ref, seg_ref,
                 c_ref, shn_ref, sn_ref,
                 carry_ref, st_ref, r_s, w_s, k_s, kk_s, b_s, vt_s, yt_s, *, n_valid):
    tt = SCAN_TILE

    @pl.when(pl.program_id(1) == 0)
    def _():
        carry_ref[...] = sh_ref[0]
        st_ref[...] = s0_ref[0]

    z = z_ref[0]
    rowi = lax.broadcasted_iota(jnp.int32, z.shape, 0)
    prev = jnp.where(rowi == 0, carry_ref[...], pltpu.roll(z, 1, 0))
    shn_ref[0] = z[n_valid - 1:n_valid, :]
    carry_ref[...] = z[tt - 1:tt, :]
    zs = z + (prev - z) * mu_ref[...]

    def segsum(x):
        return jnp.dot(x, seg_ref[...], precision=HI, preferred_element_type=F32)

    r = zs[:, 0:RWKV_W]
    k = zs[:, RWKV_W:2 * RWKV_W]
    v = zs[:, 2 * RWKV_W:3 * RWKV_W]
    wa = zs[:, 3 * RWKV_W:3 * RWKV_W + W_LORA + A_LORA]
    gd = zs[:, 3 * RWKV_W + W_LORA + A_LORA:]
    wx = w0_ref[...] + jnp.dot(jnp.tanh(wa), w2_ref[...], precision=HI, preferred_element_type=F32)
    softplus = jnp.maximum(-wx, 0.0) + jnp.log1p(jnp.exp(-jnp.abs(wx)))
    w_log = -softplus - 0.5
    decay = jnp.exp(-jnp.exp(w_log))
    a = jax.nn.sigmoid(a0_ref[...] + jnp.dot(wa, a2_ref[...], precision=HI, preferred_element_type=F32))
    g = jnp.dot(jax.nn.sigmoid(gd), g2_ref[...], precision=HI, preferred_element_type=F32)
    kk = k * kk_ref[...]
    kk = kk * lax.rsqrt(segsum(kk * kk) + 1e-12)
    k = k * (1.0 + (a - 1.0) * ka_ref[...])
    bonus = segsum(r * k * rk_ref[...]) * v

    r_s[...] = r
    w_s[...] = decay
    k_s[...] = k
    kk_s[...] = kk
    b_s[...] = kk * a
    vt_s[...] = v.T
    yt_s[...] = jnp.zeros(yt_s.shape, F32)

    lane = lax.broadcasted_iota(jnp.int32, (RWKV_DH, 2 * RWKV_DH), 1)
    lo = lane < RWKV_DH

    def pair_sum(x):
        s_lo = jnp.sum(jnp.where(lo, x, 0.0), axis=-1, keepdims=True)
        s_hi = jnp.sum(jnp.where(lo, 0.0, x), axis=-1, keepdims=True)
        return s_lo, s_hi

    def step(t, rows8, i, states):
        hot = lane == t
        new_states = []
        for p in range(2):
            sl = slice(p * 2 * RWKV_DH, (p + 1) * 2 * RWKV_DH)
            row_t = lambda name: rows8[name][i:i + 1, sl]
            s_old = states[p]
            sa_lo, sa_hi = pair_sum(s_old * row_t("kk"))
            h0 = slice(2 * p * RWKV_DH, (2 * p + 1) * RWKV_DH)
            h1 = slice((2 * p + 1) * RWKV_DH, (2 * p + 2) * RWKV_DH)
            v_lo = jnp.sum(jnp.where(hot, vt_s[h0, :], 0.0), axis=-1, keepdims=True)
            v_hi = jnp.sum(jnp.where(hot, vt_s[h1, :], 0.0), axis=-1, keepdims=True)
            s_new = (s_old * row_t("w") - jnp.where(lo, sa_lo, sa_hi) * row_t("b")
                     + jnp.where(lo, v_lo, v_hi) * row_t("k"))
            y_lo, y_hi = pair_sum(s_new * row_t("r"))
            yt_s[h0, :] = jnp.where(hot, y_lo, yt_s[h0, :])
            yt_s[h1, :] = jnp.where(hot, y_hi, yt_s[h1, :])
            new_states.append(s_new)
        return tuple(new_states)

    sub = 8

    def group(gi, states):
        base = pl.multiple_of(gi * sub, sub)
        rows8 = {name: ref[pl.ds(base, sub), :]
                 for name, ref in (("r", r_s), ("w", w_s), ("k", k_s), ("kk", kk_s), ("b", b_s))}
        for i in range(sub):
            states = step(base + i, rows8, i, states)
        return states

    final = lax.fori_loop(0, n_valid // sub, group, (st_ref[0], st_ref[1]))
    st_ref[0] = final[0]
    st_ref[1] = final[1]
    sn_ref[0] = st_ref[...]

    y = yt_s[...].T
    mean = segsum(y) * (1.0 / RWKV_DH)
    d = y - mean
    var = segsum(d * d) * (1.0 / RWKV_DH)
    y = d * lax.rsqrt(var + RWKV_GN_EPS) * lng_ref[...] + lnb_ref[...]
    c_ref[0] = (y + bonus) * g


def _rwkv(z, shift, state, mu, w0, w2p, a0, a2p, g2, kk, ka, rk, lng, lnb, seg, *, n_valid):
    b, t, _ = z.shape
    tt = SCAN_TILE
    vec = lambda w: _const_spec((1, w))
    return pl.pallas_call(
        functools.partial(_rwkv_kernel, n_valid=n_valid),
        grid=(b, t // tt),
        in_specs=[pl.BlockSpec((1, tt, RWKV_IN), lambda i, j: (i, j, 0)),
                  pl.BlockSpec((1, 1, RWKV_IN), lambda i, j: (i, 0, 0)),
                  pl.BlockSpec((1, 2, RWKV_DH, 2 * RWKV_DH), lambda i, j: (i, 0, 0, 0)),
                  vec(RWKV_IN), vec(RWKV_W), _const_spec((W_LORA + A_LORA, RWKV_W)), vec(RWKV_W),
                  _const_spec((W_LORA + A_LORA, RWKV_W)), _const_spec((G_LORA, RWKV_W)),
                  vec(RWKV_W), vec(RWKV_W), vec(RWKV_W), vec(RWKV_W), vec(RWKV_W),
                  _const_spec((RWKV_W, RWKV_W))],
        out_specs=[pl.BlockSpec((1, tt, RWKV_W), lambda i, j: (i, j, 0)),
                   pl.BlockSpec((1, 1, RWKV_IN), lambda i, j: (i, 0, 0)),
                   pl.BlockSpec((1, 2, RWKV_DH, 2 * RWKV_DH), lambda i, j: (i, 0, 0, 0))],
        out_shape=[jax.ShapeDtypeStruct((b, t, RWKV_W), F32),
                   jax.ShapeDtypeStruct((b, 1, RWKV_IN), F32),
                   jax.ShapeDtypeStruct((b, 2, RWKV_DH, 2 * RWKV_DH), F32)],
        scratch_shapes=[pltpu.VMEM((1, RWKV_IN), F32), pltpu.VMEM((2, RWKV_DH, 2 * RWKV_DH), F32)]
                       + [pltpu.VMEM((tt, RWKV_W), F32)] * 5
                       + [pltpu.VMEM((RWKV_W, tt), F32)] * 2,
        compiler_params=_params(("parallel", "arbitrary")),
        name="rwkv7",
    )(z, shift, state, mu, w0, w2p, a0, a2p, g2, kk, ka, rk, lng, lnb, seg)


def _pair_state(s):
    b = s.shape[0]
    s = s.reshape(b, 2, 2, RWKV_DH, RWKV_DH)
    return jnp.transpose(s, (0, 1, 3, 2, 4)).reshape(b, 2, RWKV_DH, 2 * RWKV_DH)


def _unpair_state(s):
    b = s.shape[0]
    s = s.reshape(b, 2, RWKV_DH, 2, RWKV_DH)
    return jnp.transpose(s, (0, 1, 3, 2, 4)).reshape(b, RWKV_HEADS, RWKV_DH, RWKV_DH)


def _out_ffn_kernel(x_ref, a_ref, b_ref, c_ref, gs_ref, wo_ref, g2_ref, wg_ref, wu_ref, wd_ref,
                    fg_ref, o_ref, acc_ref, *, final):
    mix = jnp.concatenate([a_ref[...], b_ref[...], c_ref[...]], axis=-1) * gs_ref[...]
    x1 = x_ref[...] + jnp.dot(mix.astype(BF16), wo_ref[...], preferred_element_type=F32)
    acc_ref[...] = x1
    h2 = _rms(x1, g2_ref[...]).astype(BF16)
    for c in range(0, D_FF, FF_CHUNK):
        gate = jnp.dot(h2, wg_ref[:, c:c + FF_CHUNK], preferred_element_type=F32)
        up = jnp.dot(h2, wu_ref[:, c:c + FF_CHUNK], preferred_element_type=F32)
        act = (gate * jax.nn.sigmoid(gate) * up).astype(BF16)
        acc_ref[...] += jnp.dot(act, wd_ref[c:c + FF_CHUNK, :], preferred_element_type=F32)
    x2 = acc_ref[...]
    o_ref[...] = _rms(x2, fg_ref[...]) if final else x2


def _out_ffn(x, a, b, c, gs, wo, g2, wg, wu, wd, fg, *, tm, final):
    n = x.shape[0]
    row = lambda w: pl.BlockSpec((tm, w), lambda i: (i, 0))
    return pl.pallas_call(
        functools.partial(_out_ffn_kernel, final=final),
        grid=(n // tm,),
        in_specs=[row(D_MODEL), row(CONV_W), row(ATT_W), row(RWKV_W), _const_spec((1, D_MODEL)),
                  _const_spec((D_MODEL, D_MODEL)), _const_spec((1, D_MODEL)),
                  _const_spec((D_MODEL, D_FF)), _const_spec((D_MODEL, D_FF)),
                  _const_spec((D_FF, D_MODEL)), _const_spec((1, D_MODEL))],
        out_specs=row(D_MODEL),
        out_shape=jax.ShapeDtypeStruct((n, D_MODEL), F32),
        scratch_shapes=[pltpu.VMEM((tm, D_MODEL), F32)],
        compiler_params=_params(("parallel",)),
        name="out_ffn",
    )(x, a, b, c, gs, wo, g2, wg, wu, wd, fg)


def _rope_tables(pos):
    half = ATT_DH // 2
    inv = ROPE_THETA ** (-jnp.arange(half, dtype=F32) / half)
    ang = pos.astype(F32)[:, None] * inv[None, :]
    tile = lambda t: jnp.concatenate([t] * (HEAD_W // half), axis=-1)
    return tile(jnp.cos(ang)), tile(jnp.sin(ang))


def _rotate_half_columns(w):
    half = ATT_DH // 2
    w = w.reshape(w.shape[0], ATT_W // ATT_DH, 2, half)
    return jnp.concatenate([-w[:, :, 1], w[:, :, 0]], axis=-1).reshape(w.shape[0], ATT_W)


def kernel(x_prompt, x_sample, cache_k, cache_v, state_conv, state_shift, state_wkv, page_table, norm1_g, w_in, conv_w, conv_b, conv_ln_g, conv_ln_b, conv_pw_w, conv_pw_b, att_lambda, att_subln_g, rwkv_mu, rwkv_w0, rwkv_w2, rwkv_a0, rwkv_a2, rwkv_g2, rwkv_kk, rwkv_ka, rwkv_rk, rwkv_ln_g, rwkv_ln_b, group_scale, w_out, norm2_g, w_gate, w_up, w_down, final_g):
    bp, tp, _ = x_prompt.shape
    bs, ts, _ = x_sample.shape
    assert bp == 1 and tp % ATT_BLOCK == 0 and ts <= SCAN_TILE and ts % 8 == 0
    past_len = page_table.shape[1] * PAGE_SIZE
    groups = {
        "p": dict(b=bp, t=tp, tm=512, conv_tt=256, x=x_prompt.reshape(bp * tp, D_MODEL),
                  pos=jnp.arange(tp, dtype=jnp.int32)),
        "s": dict(b=bs, t=ts, tm=bs * ts, conv_tt=ts, x=x_sample.reshape(bs * ts, D_MODEL),
                  pos=jnp.tile(past_len + jnp.arange(ts, dtype=jnp.int32), bs)),
    }
    for grp in groups.values():
        grp["cos"], grp["sin"] = _rope_tables(grp["pos"])
    outs = {name: dict(k=[], v=[], conv=[], shift=[], wkv=[]) for name in groups}

    seg = jnp.asarray(np.kron(np.eye(RWKV_HEADS, dtype=np.float32),
                              np.ones((RWKV_DH, RWKV_DH), np.float32)))
    row = lambda p: p.reshape(1, -1)

    for l in range(DEPTH):
        lam_init = 0.8 - 0.6 * math.exp(-0.3 * l)
        att_w = w_in[l][:, 2 * CONV_W:2 * CONV_W + 2 * ATT_W]
        w_ext = jnp.concatenate([w_in[l], _rotate_half_columns(att_w[:, :ATT_W]),
                                 _rotate_half_columns(att_w[:, ATT_W:])], axis=-1).astype(BF16)
        w2p = jnp.concatenate([rwkv_w2[l], jnp.zeros((A_LORA, RWKV_W), F32)], axis=0)
        a2p = jnp.concatenate([jnp.zeros((W_LORA, RWKV_W), F32), rwkv_a2[l]], axis=0)
        subln = row(att_subln_g[l])
        for name, grp in groups.items():
            b, t = grp["b"], grp["t"]
            prompt = name == "p"
            zc, q, k, v, zr, *kv_bf = _in_proj(grp["x"], row(norm1_g[l]), w_ext, grp["cos"], grp["sin"],
                                               tm=grp["tm"], emit_bf16=prompt)
            if prompt:
                conv_state = jnp.zeros((b, CONV_HALO, CONV_W), F32)
            else:
                conv_state = jnp.pad(state_conv[l], ((0, 0), (CONV_HALO - (CONV_K - 1), 0), (0, 0)))
            a_out, conv_new = _conv(zc.reshape(b, t, 2 * CONV_W), conv_state, conv_w[l].reshape(CONV_K, CONV_W),
                                    row(conv_b[l]), row(conv_ln_g[l]), row(conv_ln_b[l]),
                                    conv_pw_w[l].astype(BF16), row(conv_pw_b[l]), tt=grp["conv_tt"])
            if prompt:
                b_out = _attn_prompt(q, kv_bf[0], kv_bf[1], att_lambda[l], subln, lam_init=lam_init)
            else:
                b_out = _attn_sample(q, k, v, cache_k, cache_v, page_table, att_lambda[l], subln,
                                     layer=l, lam_init=lam_init, t_new=t)
            zr3 = zr.reshape(b, t, RWKV_IN)
            if prompt:
                shift0 = jnp.zeros((b, 1, RWKV_IN), F32)
                wkv0 = jnp.zeros((b, 2, RWKV_DH, 2 * RWKV_DH), F32)
            else:
                zr3 = jnp.pad(zr3, ((0, 0), (0, SCAN_TILE - t), (0, 0)))
                shift0 = state_shift[l].reshape(b, 1, RWKV_IN)
                wkv0 = _pair_state(state_wkv[l])
            c_out, shift_new, wkv_new = _rwkv(
                zr3, shift0, wkv0, row(rwkv_mu[l]), row(rwkv_w0[l]), w2p, row(rwkv_a0[l]), a2p,
                rwkv_g2[l], row(rwkv_kk[l]), row(rwkv_ka[l]), row(rwkv_rk[l]), row(rwkv_ln_g[l]),
                row(rwkv_ln_b[l]), seg, n_valid=min(t, SCAN_TILE))
            c_out = c_out[:, :t].reshape(b * t, RWKV_W)
            grp["x"] = _out_ffn(grp["x"], a_out.reshape(b * t, CONV_W), b_out, c_out, row(group_scale[l]),
                                w_out[l].astype(BF16), row(norm2_g[l]), w_gate[l].astype(BF16),
                                w_up[l].astype(BF16), w_down[l].astype(BF16), row(final_g),
                                tm=grp["tm"], final=(l == DEPTH - 1))
            o = outs[name]
            o["k"].append(k.reshape(b, t, ATT_HEADS, HEAD_W))
            o["v"].append(v.reshape(b, t, ATT_HEADS, HEAD_W))
            o["conv"].append(conv_new[:, CONV_HALO - (CONV_K - 1):])
            o["shift"].append(shift_new.reshape(b, RWKV_IN))
            o["wkv"].append(_unpair_state(wkv_new))

    y_prompt = groups["p"]["x"].reshape(bp, tp, D_MODEL)
    y_sample = groups["s"]["x"].reshape(bs, ts, D_MODEL)
    st = lambda name, key: jnp.stack(outs[name][key])
    return (y_prompt, y_sample,
            st("p", "k"), st("p", "v"), st("p", "conv"), st("p", "shift"), st("p", "wkv"),
            st("s", "k"), st("s", "v"), st("s", "conv"), st("s", "shift"), st("s", "wkv"))
```

```python
import functools
import math

import jax
import jax.numpy as jnp
import numpy as np
from jax import lax
from jax.experimental import pallas as pl
from jax.experimental.pallas import tpu as pltpu

D_MODEL = 1024
DEPTH = 2
PAGE_SIZE = 128
CONV_W = 256
ATT_W = 512
RWKV_W = 256
ATT_HEADS = 4
ATT_DH = 64
HEAD_W = 2 * ATT_DH
RWKV_DH = 64
RWKV_HEADS = 4
CONV_K = 31
W_LORA = 64
A_LORA = 64
G_LORA = 128
RWKV_IN = 3 * RWKV_W + W_LORA + A_LORA + G_LORA
IN_W = 2 * CONV_W + 3 * ATT_W + RWKV_IN
D_FF = 2816
ROPE_THETA = 10000.0
NORM_EPS = 1e-6
LN_EPS = 1e-5
RWKV_GN_EPS = 64e-5
NEG_INF = -1e30

F32 = jnp.float32
BF16 = jnp.bfloat16
HI = lax.Precision.HIGHEST

CONV_HALO = 32
CONV_ROWS = 64
FF_CHUNK = 256
SCAN_TILE = 128
ATT_BLOCK = 512
LOG2E = math.log2(math.e)
PAGES_PER_STEP = 8
VMEM_LIMIT = 56 * 1024 * 1024


def _const_spec(shape):
    zeros = (0,) * len(shape)
    return pl.BlockSpec(shape, lambda *_: zeros, pipeline_mode=pl.Buffered(1))


def _params(sem):
    return pltpu.CompilerParams(dimension_semantics=sem, vmem_limit_bytes=VMEM_LIMIT)


def _rms(x, g):
    return x * lax.rsqrt(jnp.mean(x * x, axis=-1, keepdims=True) + NORM_EPS) * g


def _in_proj_kernel(x_ref, g_ref, w_ref, cos_ref, sin_ref, *out_refs, emit_bf16):
    if emit_bf16:
        zc_ref, q_ref, k_ref, v_ref, zr_ref, kb_ref, vb_ref = out_refs
    else:
        zc_ref, q_ref, k_ref, v_ref, zr_ref = out_refs
    h = _rms(x_ref[...], g_ref[...]).astype(BF16)

    def mm(c0, c1):
        return jnp.dot(h, w_ref[:, c0:c1], preferred_element_type=F32)

    cos = jnp.concatenate([cos_ref[...]] * ATT_HEADS, axis=-1)
    sin = jnp.concatenate([sin_ref[...]] * ATT_HEADS, axis=-1)
    q0, k0, v0, r0 = 2 * CONV_W, 2 * CONV_W + ATT_W, 2 * CONV_W + 2 * ATT_W, 2 * CONV_W + 3 * ATT_W
    zc_ref[...] = mm(0, q0)
    q = mm(q0, k0) * cos + mm(IN_W, IN_W + ATT_W) * sin
    q_ref[...] = q * (ATT_DH ** -0.5)
    k = mm(k0, v0) * cos + mm(IN_W + ATT_W, IN_W + 2 * ATT_W) * sin
    k_ref[...] = k
    v = mm(v0, r0)
    v_ref[...] = v
    zr_ref[...] = mm(r0, IN_W)
    if emit_bf16:
        kb_ref[0] = k.astype(BF16)
        vb_ref[0] = v.T.astype(BF16)


def _in_proj(x, g, w_ext, cos, sin, *, tm, emit_bf16):
    n = x.shape[0]
    row = lambda w: pl.BlockSpec((tm, w), lambda i: (i, 0))
    widths = [2 * CONV_W, ATT_W, ATT_W, ATT_W, RWKV_IN]
    out_shape = [jax.ShapeDtypeStruct((n, w), F32) for w in widths]
    out_specs = [row(w) for w in widths]
    if emit_bf16:
        assert tm == ATT_BLOCK
        out_shape += [jax.ShapeDtypeStruct((n // tm, tm, ATT_W), BF16),
                      jax.ShapeDtypeStruct((n // tm, ATT_W, tm), BF16)]
        out_specs += [pl.BlockSpec((1, tm, ATT_W), lambda i: (i, 0, 0)),
                      pl.BlockSpec((1, ATT_W, tm), lambda i: (i, 0, 0))]
    return pl.pallas_call(
        functools.partial(_in_proj_kernel, emit_bf16=emit_bf16),
        grid=(n // tm,),
        in_specs=[row(D_MODEL), _const_spec((1, D_MODEL)), _const_spec(w_ext.shape),
                  row(HEAD_W), row(HEAD_W)],
        out_specs=out_specs,
        out_shape=out_shape,
        compiler_params=_params(("parallel",)),
        name="in_proj",
    )(x, g, w_ext, cos, sin)


def _conv_kernel(zc_ref, st_ref, cw_ref, cb_ref, lg_ref, lb_ref, pw_ref, pb_ref,
                 a_ref, cn_ref, pad_ref, *, tt):
    @pl.when(pl.program_id(1) == 0)
    def _():
        pad_ref[0:CONV_HALO, :] = st_ref[0]

    z = zc_ref[0]
    pad_ref[CONV_HALO:CONV_HALO + tt, :] = z[:, :CONV_W] * jax.nn.sigmoid(z[:, CONV_W:])
    rc = min(tt, CONV_ROWS)
    first = CONV_HALO - (CONV_K - 1)
    for r0 in range(0, tt, rc):
        acc = jnp.zeros((rc, CONV_W), F32)
        for j in range(CONV_K):
            acc = acc + cw_ref[j:j + 1, :] * pad_ref[r0 + first + j:r0 + first + j + rc, :]
        y = acc + cb_ref[...]
        mu = jnp.mean(y, axis=-1, keepdims=True)
        d = y - mu
        var = jnp.mean(d * d, axis=-1, keepdims=True)
        y = d * lax.rsqrt(var + LN_EPS) * lg_ref[...] + lb_ref[...]
        y = (y * jax.nn.sigmoid(y)).astype(BF16)
        a_ref[0, r0:r0 + rc, :] = jnp.dot(y, pw_ref[...], preferred_element_type=F32) + pb_ref[...]
    tail = pad_ref[tt:tt + CONV_HALO, :]
    cn_ref[0] = tail
    pad_ref[0:CONV_HALO, :] = tail


def _conv(zc, state, cw, cb, lg, lb, pw, pb, *, tt):
    b, t, _ = zc.shape
    return pl.pallas_call(
        functools.partial(_conv_kernel, tt=tt),
        grid=(b, t // tt),
        in_specs=[pl.BlockSpec((1, tt, 2 * CONV_W), lambda i, j: (i, j, 0)),
                  pl.BlockSpec((1, CONV_HALO, CONV_W), lambda i, j: (i, 0, 0)),
                  _const_spec((CONV_K, CONV_W)), _const_spec((1, CONV_W)), _const_spec((1, CONV_W)),
                  _const_spec((1, CONV_W)), _const_spec((CONV_W, CONV_W)), _const_spec((1, CONV_W))],
        out_specs=[pl.BlockSpec((1, tt, CONV_W), lambda i, j: (i, j, 0)),
                   pl.BlockSpec((1, CONV_HALO, CONV_W), lambda i, j: (i, 0, 0))],
        out_shape=[jax.ShapeDtypeStruct((b, t, CONV_W), F32),
                   jax.ShapeDtypeStruct((b, CONV_HALO, CONV_W), F32)],
        scratch_shapes=[pltpu.VMEM((CONV_HALO + tt, CONV_W), F32)],
        compiler_params=_params(("parallel", "arbitrary")),
        name="conv_module",
    )(zc, state, cw, cb, lg, lb, pw, pb)


def _lam(lam_ref, lam_init):
    lp = lam_ref[...]
    s1 = jnp.sum(lp[0:1] * lp[1:2], axis=-1, keepdims=True)
    s2 = jnp.sum(lp[2:3] * lp[3:4], axis=-1, keepdims=True)
    return jnp.exp(s1) - jnp.exp(s2) + lam_init


def _diff_combine(acc0, l0, acc1, l1, lam, g, lam_init):
    o = acc0 / l0 - lam * (acc1 / l1)
    return _rms(o, g) * (1.0 - lam_init)


def _attn_prompt_kernel(q_ref, k_ref, vt_ref, lam_ref, g_ref, o_ref, acc_ref, sa_ref, sb_ref, *, lam_init):
    qi = pl.program_id(1)
    tq = ATT_BLOCK
    qt = (q_ref[...] * LOG2E).T
    first = lax.broadcasted_iota(jnp.int32, qt.shape, 0) < ATT_DH
    qc = (jnp.where(first, qt, 0.0).astype(BF16), jnp.where(first, 0.0, qt).astype(BF16))
    acc_ref[...] = jnp.zeros(acc_ref.shape, F32)

    def logits(kc, s_ref):
        kj = k_ref[kc]
        for c in range(2):
            s_ref[c] = jnp.dot(kj, qc[c], preferred_element_type=F32)

    def update(kc, s_ref, carry, masked):
        vj = vt_ref[kc]
        out = []
        for c in range(2):
            m_old, l_old = carry[c]
            s = s_ref[c]
            if masked:
                key = lax.broadcasted_iota(jnp.int32, s.shape, 0) + (kc - qi) * tq
                qry = lax.broadcasted_iota(jnp.int32, s.shape, 1)
                s = jnp.where(key <= qry, s, NEG_INF)
            m_new = jnp.maximum(m_old, jnp.max(s, axis=0, keepdims=True))
            alpha = jnp.exp2(m_old - m_new)
            p = jnp.exp2(s - m_new)
            l_new = alpha * l_old + jnp.sum(p, axis=0, keepdims=True)
            acc_ref[c] = alpha * acc_ref[c] + jnp.dot(vj, p.astype(BF16), preferred_element_type=F32)
            out.append((m_new, l_new))
        return tuple(out)

    last = qi // 2

    def pair(j, carry):
        logits(2 * j + 1, sb_ref)
        carry = update(2 * j, sa_ref, carry, False)
        logits(2 * j + 2, sa_ref)
        return update(2 * j + 1, sb_ref, carry, False)

    logits(0, sa_ref)
    init = ((jnp.full((1, tq), NEG_INF, F32), jnp.zeros((1, tq), F32)),) * 2
    carry = lax.fori_loop(0, last, pair, init)
    logits(2 * last + 1, sb_ref)
    carry = update(2 * last, sa_ref, carry, True)
    (_, l0), (_, l1) = update(2 * last + 1, sb_ref, carry, True)
    o_t = acc_ref[0] / l0 - _lam(lam_ref, lam_init) * (acc_ref[1] / l1)
    o_ref[...] = _rms(o_t.T, g_ref[...]) * (1.0 - lam_init)


def _attn_prompt(q, kb, vtb, lam_p, g, *, lam_init):
    n = q.shape[0]
    nc = n // ATT_BLOCK
    return pl.pallas_call(
        functools.partial(_attn_prompt_kernel, lam_init=lam_init),
        grid=(ATT_HEADS, nc),
        in_specs=[pl.BlockSpec((ATT_BLOCK, HEAD_W), lambda h, i: (i, h)),
                  pl.BlockSpec((nc, ATT_BLOCK, HEAD_W), lambda h, i: (0, 0, h)),
                  pl.BlockSpec((nc, HEAD_W, ATT_BLOCK), lambda h, i: (0, h, 0)),
                  pl.BlockSpec((4, ATT_DH), lambda h, i: (0, 0)),
                  pl.BlockSpec((1, HEAD_W), lambda h, i: (0, 0))],
        out_specs=pl.BlockSpec((ATT_BLOCK, HEAD_W), lambda h, i: (i, h)),
        out_shape=jax.ShapeDtypeStruct((n, ATT_W), F32),
        scratch_shapes=[pltpu.VMEM((2, HEAD_W, ATT_BLOCK), F32)]
                       + [pltpu.VMEM((2, ATT_BLOCK, ATT_BLOCK), F32)] * 2,
        compiler_params=_params(("parallel", "arbitrary")),
        name="attn_prompt",
    )(q, kb, vtb, lam_p, g)


def _attn_sample_kernel(pt_ref, q_ref, kn_ref, vn_ref, *rest, lam_init, t_new):
    del pt_ref
    k_refs = rest[:PAGES_PER_STEP]
    v_refs = rest[PAGES_PER_STEP:2 * PAGES_PER_STEP]
    lam_ref, g_ref, o_ref, m_ref, l_ref, acc_ref = rest[2 * PAGES_PER_STEP:]
    step = pl.program_id(1)
    rows = ATT_HEADS * 2 * t_new
    page_cols = PAGE_SIZE * ATT_HEADS

    q = q_ref[...]
    first = lax.broadcasted_iota(jnp.int32, (t_new, HEAD_W), 1) < ATT_DH
    pieces = []
    for h in range(ATT_HEADS):
        qh = q[:, h * HEAD_W:(h + 1) * HEAD_W]
        pieces += [jnp.where(first, qh, 0.0), jnp.where(first, 0.0, qh)]
    qm = jnp.concatenate(pieces, axis=0).astype(BF16)

    def update(s, v):
        m_old = m_ref[...]
        m_new = jnp.maximum(m_old, jnp.max(s, axis=-1, keepdims=True))
        alpha = jnp.exp(m_old - m_new)
        p = jnp.exp(s - m_new)
        l_ref[...] = alpha * l_ref[...] + jnp.sum(p, axis=-1, keepdims=True)
        acc_ref[...] = alpha * acc_ref[...] + jnp.dot(p.astype(BF16), v, preferred_element_type=F32)
        m_ref[...] = m_new

    def logits(keys):
        return lax.dot_general(qm, keys, (((1,), (1,)), ((), ())), preferred_element_type=F32)

    @pl.when(step == 0)
    def _():
        m_ref[...] = jnp.full(m_ref.shape, NEG_INF, F32)
        l_ref[...] = jnp.zeros(l_ref.shape, F32)
        acc_ref[...] = jnp.zeros(acc_ref.shape, F32)
        zpad = jnp.zeros((PAGE_SIZE - ATT_HEADS * t_new, HEAD_W), F32)
        stack = lambda ref: jnp.concatenate(
            [ref[:, h * HEAD_W:(h + 1) * HEAD_W] for h in range(ATT_HEADS)] + [zpad], axis=0).astype(BF16)
        col = lax.broadcasted_iota(jnp.int32, (rows, PAGE_SIZE), 1)
        row = lax.broadcasted_iota(jnp.int32, (rows, PAGE_SIZE), 0)
        valid = (col // t_new == row // (2 * t_new)) & (col % t_new <= row % t_new)
        update(jnp.where(valid, logits(stack(kn_ref)), NEG_INF), stack(vn_ref))

    col = lax.broadcasted_iota(jnp.int32, (rows, page_cols), 1)
    row = lax.broadcasted_iota(jnp.int32, (rows, page_cols), 0)
    bias = jnp.where(col % ATT_HEADS == row // (2 * t_new), 0.0, NEG_INF)
    bias = jnp.concatenate([bias] * PAGES_PER_STEP, axis=1)
    keys = jnp.concatenate([r[...] for r in k_refs], axis=0).astype(BF16)
    vals = jnp.concatenate([r[...] for r in v_refs], axis=0).astype(BF16)
    update(logits(keys) + bias, vals)

    @pl.when(step == pl.num_programs(1) - 1)
    def _():
        lam = _lam(lam_ref, lam_init)
        acc = acc_ref[...]
        l = l_ref[...]
        outs = []
        for h in range(ATT_HEADS):
            r0, r1, r2 = 2 * h * t_new, (2 * h + 1) * t_new, (2 * h + 2) * t_new
            outs.append(_diff_combine(acc[r0:r1], l[r0:r1], acc[r1:r2], l[r1:r2], lam, g_ref[...], lam_init))
        o_ref[...] = jnp.concatenate(outs, axis=-1)


def _attn_sample(q, k_new, v_new, cache_k, cache_v, page_table, lam_p, g, *, layer, lam_init, t_new):
    n = q.shape[0]
    b = n // t_new
    n_pages = page_table.shape[1]
    steps = n_pages // PAGES_PER_STEP
    pool = cache_k.shape[1]
    ck = cache_k.reshape(DEPTH, pool, PAGE_SIZE * ATT_HEADS, HEAD_W)
    cv = cache_v.reshape(DEPTH, pool, PAGE_SIZE * ATT_HEADS, HEAD_W)
    row = pl.BlockSpec((t_new, ATT_W), lambda i, s, pt: (i, 0))

    def page_spec(j):
        return pl.BlockSpec((None, None, PAGE_SIZE * ATT_HEADS, HEAD_W),
                            lambda i, s, pt: (layer, pt[i, s * PAGES_PER_STEP + j], 0, 0))

    pages = [page_spec(j) for j in range(PAGES_PER_STEP)]
    rows = ATT_HEADS * 2 * t_new
    grid_spec = pltpu.PrefetchScalarGridSpec(
        num_scalar_prefetch=1,
        grid=(b, steps),
        in_specs=[row, row, row] + pages + pages + [
            pl.BlockSpec((4, ATT_DH), lambda i, s, pt: (0, 0)),
            pl.BlockSpec((1, HEAD_W), lambda i, s, pt: (0, 0))],
        out_specs=row,
        scratch_shapes=[pltpu.VMEM((rows, 1), F32), pltpu.VMEM((rows, 1), F32),
                        pltpu.VMEM((rows, HEAD_W), F32)],
    )
    return pl.pallas_call(
        functools.partial(_attn_sample_kernel, lam_init=lam_init, t_new=t_new),
        grid_spec=grid_spec,
        out_shape=jax.ShapeDtypeStruct((n, ATT_W), F32),
        compiler_params=_params(("parallel", "arbitrary")),
        name="attn_sample",
    )(page_table, q, k_new, v_new, *([ck] * PAGES_PER_STEP), *([cv] * PAGES_PER_STEP), lam_p, g)


def _rwkv_kernel(z_ref, sh_ref, s0_ref, mu_ref, w0_ref, w2_ref, a0_ref, a2_ref, g2_ref,
                 kk_ref, ka_ref, rk_ref, lng_ref, lnb_ref, seg_ref,
                 c_ref, shn_ref, sn_ref,
                 carry_ref, st_ref, r_s, w_s, k_s, kk_s, b_s, vt_s, yt_s, *, n_valid):
    tt = SCAN_TILE

    @pl.when(pl.program_id(1) == 0)
    def _():
        carry_ref[...] = sh_ref[0]
        st_ref[...] = s0_ref[0]

    z = z_ref[0]
    rowi = lax.broadcasted_iota(jnp.int32, z.shape, 0)
    prev = jnp.where(rowi == 0, carry_ref[...], pltpu.roll(z, 1, 0))
    shn_ref[0] = z[n_valid - 1:n_valid, :]
    carry_ref[...] = z[tt - 1:tt, :]
    zs = z + (prev - z) * mu_ref[...]

    def segsum(x):
        return jnp.dot(x, seg_ref[...], precision=HI, preferred_element_type=F32)

    r = zs[:, 0:RWKV_W]
    k = zs[:, RWKV_W:2 * RWKV_W]
    v = zs[:, 2 * RWKV_W:3 * RWKV_W]
    wa = zs[:, 3 * RWKV_W:3 * RWKV_W + W_LORA + A_LORA]
    gd = zs[:, 3 * RWKV_W + W_LORA + A_LORA:]
    wx = w0_ref[...] + jnp.dot(jnp.tanh(wa), w2_ref[...], precision=HI, preferred_element_type=F32)
    softplus = jnp.maximum(-wx, 0.0) + jnp.log1p(jnp.exp(-jnp.abs(wx)))
    w_log = -softplus - 0.5
    decay = jnp.exp(-jnp.exp(w_log))
    a = jax.nn.sigmoid(a0_ref[...] + jnp.dot(wa, a2_ref[...], precision=HI, preferred_element_type=F32))
    g = jnp.dot(jax.nn.sigmoid(gd), g2_ref[...], precision=HI, preferred_element_type=F32)
    kk = k * kk_ref[...]
    kk = kk * lax.rsqrt(segsum(kk * kk) + 1e-12)
    k = k * (1.0 + (a - 1.0) * ka_ref[...])
    bonus = segsum(r * k * rk_ref[...]) * v

    r_s[...] = r
    w_s[...] = decay
    k_s[...] = k
    kk_s[...] = kk
    b_s[...] = kk * a
    vt_s[...] = v.T
    yt_s[...] = jnp.zeros(yt_s.shape, F32)

    lane = lax.broadcasted_iota(jnp.int32, (RWKV_DH, 2 * RWKV_DH), 1)
    lo = lane < RWKV_DH

    def pair_sum(x):
        s_lo = jnp.sum(jnp.where(lo, x, 0.0), axis=-1, keepdims=True)
        s_hi = jnp.sum(jnp.where(lo, 0.0, x), axis=-1, keepdims=True)
        return s_lo, s_hi

    def step(t, rows8, i, states):
        hot = lane == t
        new_states = []
        for p in range(2):
            sl = slice(p * 2 * RWKV_DH, (p + 1) * 2 * RWKV_DH)
            row_t = lambda name: rows8[name][i:i + 1, sl]
            s_old = states[p]
            sa_lo, sa_hi = pair_sum(s_old * row_t("kk"))
            h0 = slice(2 * p * RWKV_DH, (2 * p + 1) * RWKV_DH)
            h1 = slice((2 * p + 1) * RWKV_DH, (2 * p + 2) * RWKV_DH)
            v_lo = jnp.sum(jnp.where(hot, vt_s[h0, :], 0.0), axis=-1, keepdims=True)
            v_hi = jnp.sum(jnp.where(hot, vt_s[h1, :], 0.0), axis=-1, keepdims=True)
            s_new = (s_old * row_t("w") - jnp.where(lo, sa_lo, sa_hi) * row_t("b")
                     + jnp.where(lo, v_lo, v_hi) * row_t("k"))
            y_lo, y_hi = pair_sum(s_new * row_t("r"))
            yt_s[h0, :] = jnp.where(hot, y_lo, yt_s[h0, :])
            yt_s[h1, :] = jnp.where(hot, y_hi, yt_s[h1, :])
            new_states.append(s_new)
        return tuple(new_states)

    sub = 8

    def group(gi, states):
        base = pl.multiple_of(gi * sub, sub)
        rows8 = {name: ref[pl.ds(base, sub), :]
                 for name, ref in (("r", r_s), ("w", w_s), ("k", k_s), ("kk", kk_s), ("b", b_s))}
        for i in range(sub):
            states = step(base + i, rows8, i, states)
        return states

    final = lax.fori_loop(0, n_valid // sub, group, (st_ref[0], st_ref[1]))
    st_ref[0] = final[0]
    st_ref[1] = final[1]
    sn_ref[0] = st_ref[...]

    y = yt_s[...].T
    mean = segsum(y) * (1.0 / RWKV_DH)
    d = y - mean
    var = segsum(d * d) * (1.0 / RWKV_DH)
    y = d * lax.rsqrt(var + RWKV_GN_EPS) * lng_ref[...] + lnb_ref[...]
    c_ref[0] = (y + bonus) * g


def _rwkv(z, shift, state, mu, w0, w2p, a0, a2p, g2, kk, ka, rk, lng, lnb, seg, *, n_valid):
    b, t, _ = z.shape
    tt = SCAN_TILE
    vec = lambda w: _const_spec((1, w))
    return pl.pallas_call(
        functools.partial(_rwkv_kernel, n_valid=n_valid),
        grid=(b, t // tt),
        in_specs=[pl.BlockSpec((1, tt, RWKV_IN), lambda i, j: (i, j, 0)),
                  pl.BlockSpec((1, 1, RWKV_IN), lambda i, j: (i, 0, 0)),
                  pl.BlockSpec((1, 2, RWKV_DH, 2 * RWKV_DH), lambda i, j: (i, 0, 0, 0)),
                  vec(RWKV_IN), vec(RWKV_W), _const_spec((W_LORA + A_LORA, RWKV_W)), vec(RWKV_W),
                  _const_spec((W_LORA + A_LORA, RWKV_W)), _const_spec((G_LORA, RWKV_W)),
                  vec(RWKV_W), vec(RWKV_W), vec(RWKV_W), vec(RWKV_W), vec(RWKV_W),
                  _const_spec((RWKV_W, RWKV_W))],
        out_specs=[pl.BlockSpec((1, tt, RWKV_W), lambda i, j: (i, j, 0)),
                   pl.BlockSpec((1, 1, RWKV_IN), lambda i, j: (i, 0, 0)),
                   pl.BlockSpec((1, 2, RWKV_DH, 2 * RWKV_DH), lambda i, j: (i, 0, 0, 0))],
        out_shape=[jax.ShapeDtypeStruct((b, t, RWKV_W), F32),
                   jax.ShapeDtypeStruct((b, 1, RWKV_IN), F32),
                   jax.ShapeDtypeStruct((b, 2, RWKV_DH, 2 * RWKV_DH), F32)],
        scratch_shapes=[pltpu.VMEM((1, RWKV_IN), F32), pltpu.VMEM((2, RWKV_DH, 2 * RWKV_DH), F32)]
                       + [pltpu.VMEM((tt, RWKV_W), F32)] * 5
                       + [pltpu.VMEM((RWKV_W, tt), F32)] * 2,
        compiler_params=_params(("parallel", "arbitrary")),
        name="rwkv7",
    )(z, shift, state, mu, w0, w2p, a0, a2p, g2, kk, ka, rk, lng, lnb, seg)


def _pair_state(s):
    b = s.shape[0]
    s = s.reshape(b, 2, 2, RWKV_DH, RWKV_DH)
    return jnp.transpose(s, (0, 1, 3, 2, 4)).reshape(b, 2, RWKV_DH, 2 * RWKV_DH)


def _unpair_state(s):
    b = s.shape[0]
    s = s.reshape(b, 2, RWKV_DH, 2, RWKV_DH)
    return jnp.transpose(s, (0, 1, 3, 2, 4)).reshape(b, RWKV_HEADS, RWKV_DH, RWKV_DH)


def _out_ffn_kernel(x_ref, a_ref, b_ref, c_ref, gs_ref, wo_ref, g2_ref, wg_ref, wu_ref, wd_ref,
                    fg_ref, o_ref, acc_ref, *, final):
    mix = jnp.concatenate([a_ref[...], b_ref[...], c_ref[...]], axis=-1) * gs_ref[...]
    x1 = x_ref[...] + jnp.dot(mix.astype(BF16), wo_ref[...], preferred_element_type=F32)
    acc_ref[...] = x1
    h2 = _rms(x1, g2_ref[...]).astype(BF16)
    for c in range(0, D_FF, FF_CHUNK):
        gate = jnp.dot(h2, wg_ref[:, c:c + FF_CHUNK], preferred_element_type=F32)
        up = jnp.dot(h2, wu_ref[:, c:c + FF_CHUNK], preferred_element_type=F32)
        act = (gate * jax.nn.sigmoid(gate) * up).astype(BF16)
        acc_ref[...] += jnp.dot(act, wd_ref[c:c + FF_CHUNK, :], preferred_element_type=F32)
    x2 = acc_ref[...]
    o_ref[...] = _rms(x2, fg_ref[...]) if final else x2


def _out_ffn(x, a, b, c, gs, wo, g2, wg, wu, wd, fg, *, tm, final):
    n = x.shape[0]
    row = lambda w: pl.BlockSpec((tm, w), lambda i: (i, 0))
    return pl.pallas_call(
        functools.partial(_out_ffn_kernel, final=final),
        grid=(n // tm,),
        in_specs=[row(D_MODEL), row(CONV_W), row(ATT_W), row(RWKV_W), _const_spec((1, D_MODEL)),
                  _const_spec((D_MODEL, D_MODEL)), _const_spec((1, D_MODEL)),
                  _const_spec((D_MODEL, D_FF)), _const_spec((D_MODEL, D_FF)),
                  _const_spec((D_FF, D_MODEL)), _const_spec((1, D_MODEL))],
        out_specs=row(D_MODEL),
        out_shape=jax.ShapeDtypeStruct((n, D_MODEL), F32),
        scratch_shapes=[pltpu.VMEM((tm, D_MODEL), F32)],
        compiler_params=_params(("parallel",)),
        name="out_ffn",
    )(x, a, b, c, gs, wo, g2, wg, wu, wd, fg)


def _rope_tables(pos):
    half = ATT_DH // 2
    inv = ROPE_THETA ** (-jnp.arange(half, dtype=F32) / half)
    ang = pos.astype(F32)[:, None] * inv[None, :]
    tile = lambda t: jnp.concatenate([t] * (HEAD_W // half), axis=-1)
    return tile(jnp.cos(ang)), tile(jnp.sin(ang))


def _rotate_half_columns(w):
    half = ATT_DH // 2
    w = w.reshape(w.shape[0], ATT_W // ATT_DH, 2, half)
    return jnp.concatenate([-w[:, :, 1], w[:, :, 0]], axis=-1).reshape(w.shape[0], ATT_W)


def kernel(x_prompt, x_sample, cache_k, cache_v, state_conv, state_shift, state_wkv, page_table, norm1_g, w_in, conv_w, conv_b, conv_ln_g, conv_ln_b, conv_pw_w, conv_pw_b, att_lambda, att_subln_g, rwkv_mu, rwkv_w0, rwkv_w2, rwkv_a0, rwkv_a2, rwkv_g2, rwkv_kk, rwkv_ka, rwkv_rk, rwkv_ln_g, rwkv_ln_b, group_scale, w_out, norm2_g, w_gate, w_up, w_down, final_g):
    bp, tp, _ = x_prompt.shape
    bs, ts, _ = x_sample.shape
    assert bp == 1 and tp % ATT_BLOCK == 0 and ts <= SCAN_TILE and ts % 8 == 0
    past_len = page_table.shape[1] * PAGE_SIZE
    groups = {
        "p": dict(b=bp, t=tp, tm=512, conv_tt=256, x=x_prompt.reshape(bp * tp, D_MODEL),
                  pos=jnp.arange(tp, dtype=jnp.int32)),
        "s": dict(b=bs, t=ts, tm=bs * ts, conv_tt=ts, x=x_sample.reshape(bs * ts, D_MODEL),
                  pos=jnp.tile(past_len + jnp.arange(ts, dtype=jnp.int32), bs)),
    }
    for grp in groups.values():
        grp["cos"], grp["sin"] = _rope_tables(grp["pos"])
    outs = {name: dict(k=[], v=[], conv=[], shift=[], wkv=[]) for name in groups}

    seg = jnp.asarray(np.kron(np.eye(RWKV_HEADS, dtype=np.float32),
                              np.ones((RWKV_DH, RWKV_DH), np.float32)))
    row = lambda p: p.reshape(1, -1)

    for l in range(DEPTH):
        lam_init = 0.8 - 0.6 * math.exp(-0.3 * l)
        att_w = w_in[l][:, 2 * CONV_W:2 * CONV_W + 2 * ATT_W]
        w_ext = jnp.concatenate([w_in[l], _rotate_half_columns(att_w[:, :ATT_W]),
                                 _rotate_half_columns(att_w[:, ATT_W:])], axis=-1).astype(BF16)
        w2p = jnp.concatenate([rwkv_w2[l], jnp.zeros((A_LORA, RWKV_W), F32)], axis=0)
        a2p = jnp.concatenate([jnp.zeros((W_LORA, RWKV_W), F32), rwkv_a2[l]], axis=0)
        subln = row(att_subln_g[l])
        for name, grp in groups.items():
            b, t = grp["b"], grp["t"]
            prompt = name == "p"
            zc, q, k, v, zr, *kv_bf = _in_proj(grp["x"], row(norm1_g[l]), w_ext, grp["cos"], grp["sin"],
                                               tm=grp["tm"], emit_bf16=prompt)
            if prompt:
                conv_state = jnp.zeros((b, CONV_HALO, CONV_W), F32)
            else:
                conv_state = jnp.pad(state_conv[l], ((0, 0), (CONV_HALO - (CONV_K - 1), 0), (0, 0)))
            a_out, conv_new = _conv(zc.reshape(b, t, 2 * CONV_W), conv_state, conv_w[l].reshape(CONV_K, CONV_W),
                                    row(conv_b[l]), row(conv_ln_g[l]), row(conv_ln_b[l]),
                                    conv_pw_w[l].astype(BF16), row(conv_pw_b[l]), tt=grp["conv_tt"])
            if prompt:
                b_out = _attn_prompt(q, kv_bf[0], kv_bf[1], att_lambda[l], subln, lam_init=lam_init)
            else:
                b_out = _attn_sample(q, k, v, cache_k, cache_v, page_table, att_lambda[l], subln,
                                     layer=l, lam_init=lam_init, t_new=t)
            zr3 = zr.reshape(b, t, RWKV_IN)
            if prompt:
                shift0 = jnp.zeros((b, 1, RWKV_IN), F32)
                wkv0 = jnp.zeros((b, 2, RWKV_DH, 2 * RWKV_DH), F32)
            else:
                zr3 = jnp.pad(zr3, ((0, 0), (0, SCAN_TILE - t), (0, 0)))
                shift0 = state_shift[l].reshape(b, 1, RWKV_IN)
                wkv0 = _pair_state(state_wkv[l])
            c_out, shift_new, wkv_new = _rwkv(
                zr3, shift0, wkv0, row(rwkv_mu[l]), row(rwkv_w0[l]), w2p, row(rwkv_a0[l]), a2p,
                rwkv_g2[l], row(rwkv_kk[l]), row(rwkv_ka[l]), row(rwkv_rk[l]), row(rwkv_ln_g[l]),
                row(rwkv_ln_b[l]), seg, n_valid=min(t, SCAN_TILE))
            c_out = c_out[:, :t].reshape(b * t, RWKV_W)
            grp["x"] = _out_ffn(grp["x"], a_out.reshape(b * t, CONV_W), b_out, c_out, row(group_scale[l]),
                                w_out[l].astype(BF16), row(norm2_g[l]), w_gate[l].astype(BF16),
                                w_up[l].astype(BF16), w_down[l].astype(BF16), row(final_g),
                                tm=grp["tm"], final=(l == DEPTH - 1))
            o = outs[name]
            o["k"].append(k.reshape(b, t, ATT_HEADS, HEAD_W))
            o["v"].append(v.reshape(b, t, ATT_HEADS, HEAD_W))
            o["conv"].append(conv_new[:, CONV_HALO - (CONV_K - 1):])
            o["shift"].append(shift_new.reshape(b, RWKV_IN))
            o["wkv"].append(_unpair_state(wkv_new))

    y_prompt = groups["p"]["x"].reshape(bp, tp, D_MODEL)
    y_sample = groups["s"]["x"].reshape(bs, ts, D_MODEL)
    st = lambda name, key: jnp.stack(outs[name][key])
    return (y_prompt, y_sample,
            st("p", "k"), st("p", "v"), st("p", "conv"), st("p", "shift"), st("p", "wkv"),
            st("s", "k"), st("s", "v"), st("s", "conv"), st("s", "shift"), st("s", "wkv"))
```

```python
import functools
import math

import jax
import jax.numpy as jnp
import numpy as np
from jax import lax
from jax.experimental import pallas as pl
from jax.experimental.pallas import tpu as pltpu

D_MODEL = 1024
DEPTH = 2
PAGE_SIZE = 128
CONV_W = 256
ATT_W = 512
RWKV_W = 256
ATT_HEADS = 4
ATT_DH = 64
HEAD_W = 2 * ATT_DH
RWKV_DH = 64
RWKV_HEADS = 4
CONV_K = 31
W_LORA = 64
A_LORA = 64
G_LORA = 128
RWKV_IN = 3 * RWKV_W + W_LORA + A_LORA + G_LORA
IN_W = 2 * CONV_W + 3 * ATT_W + RWKV_IN
D_FF = 2816
ROPE_THETA = 10000.0
NORM_EPS = 1e-6
LN_EPS = 1e-5
RWKV_GN_EPS = 64e-5
NEG_INF = -1e30

F32 = jnp.float32
BF16 = jnp.bfloat16
HI = lax.Precision.HIGHEST

CONV_HALO = 32
CONV_ROWS = 64
FF_CHUNK = 256
SCAN_TILE = 128
ATT_BLOCK = 512
LOG2E = math.log2(math.e)
PAGES_PER_STEP = 8
VMEM_LIMIT = 56 * 1024 * 1024


def _const_spec(shape):
    zeros = (0,) * len(shape)
    return pl.BlockSpec(shape, lambda *_: zeros, pipeline_mode=pl.Buffered(1))


def _params(sem):
    return pltpu.CompilerParams(dimension_semantics=sem, vmem_limit_bytes=VMEM_LIMIT)


def _rms(x, g):
    return x * lax.rsqrt(jnp.mean(x * x, axis=-1, keepdims=True) + NORM_EPS) * g


def _in_proj_kernel(x_ref, g_ref, w_ref, cos_ref, sin_ref, *out_refs, emit_bf16):
    if emit_bf16:
        zc_ref, q_ref, k_ref, v_ref, zr_ref, kb_ref, vb_ref = out_refs
    else:
        zc_ref, q_ref, k_ref, v_ref, zr_ref = out_refs
    h = _rms(x_ref[...], g_ref[...]).astype(BF16)

    def mm(c0, c1):
        return jnp.dot(h, w_ref[:, c0:c1], preferred_element_type=F32)

    cos = jnp.concatenate([cos_ref[...]] * ATT_HEADS, axis=-1)
    sin = jnp.concatenate([sin_ref[...]] * ATT_HEADS, axis=-1)
    q0, k0, v0, r0 = 2 * CONV_W, 2 * CONV_W + ATT_W, 2 * CONV_W + 2 * ATT_W, 2 * CONV_W + 3 * ATT_W
    zc_ref[...] = mm(0, q0)
    q = mm(q0, k0) * cos + mm(IN_W, IN_W + ATT_W) * sin
    q_ref[...] = q * (ATT_DH ** -0.5)
    k = mm(k0, v0) * cos + mm(IN_W + ATT_W, IN_W + 2 * ATT_W) * sin
    k_ref[...] = k
    v = mm(v0, r0)
    v_ref[...] = v
    zr_ref[...] = mm(r0, IN_W)
    if emit_bf16:
        kb_ref[0] = k.astype(BF16)
        vb_ref[0] = v.T.astype(BF16)


def _in_proj(x, g, w_ext, cos, sin, *, tm, emit_bf16):
    n = x.shape[0]
    row = lambda w: pl.BlockSpec((tm, w), lambda i: (i, 0))
    widths = [2 * CONV_W, ATT_W, ATT_W, ATT_W, RWKV_IN]
    out_shape = [jax.ShapeDtypeStruct((n, w), F32) for w in widths]
    out_specs = [row(w) for w in widths]
    if emit_bf16:
        assert tm == ATT_BLOCK
        out_shape += [jax.ShapeDtypeStruct((n // tm, tm, ATT_W), BF16),
                      jax.ShapeDtypeStruct((n // tm, ATT_W, tm), BF16)]
        out_specs += [pl.BlockSpec((1, tm, ATT_W), lambda i: (i, 0, 0)),
                      pl.BlockSpec((1, ATT_W, tm), lambda i: (i, 0, 0))]
    return pl.pallas_call(
        functools.partial(_in_proj_kernel, emit_bf16=emit_bf16),
        grid=(n // tm,),
        in_specs=[row(D_MODEL), _const_spec((1, D_MODEL)), _const_spec(w_ext.shape),
                  row(HEAD_W), row(HEAD_W)],
        out_specs=out_specs,
        out_shape=out_shape,
        compiler_params=_params(("parallel",)),
        name="in_proj",
    )(x, g, w_ext, cos, sin)


def _conv_kernel(zc_ref, st_ref, cw_ref, cb_ref, lg_ref, lb_ref, pw_ref, pb_ref,
                 a_ref, cn_ref, pad_ref, *, tt):
    @pl.when(pl.program_id(1) == 0)
    def _():
        pad_ref[0:CONV_HALO, :] = st_ref[0]

    z = zc_ref[0]
    pad_ref[CONV_HALO:CONV_HALO + tt, :] = z[:, :CONV_W] * jax.nn.sigmoid(z[:, CONV_W:])
    rc = min(tt, CONV_ROWS)
    first = CONV_HALO - (CONV_K - 1)
    for r0 in range(0, tt, rc):
        acc = jnp.zeros((rc, CONV_W), F32)
        for j in range(CONV_K):
            acc = acc + cw_ref[j:j + 1, :] * pad_ref[r0 + first + j:r0 + first + j + rc, :]
        y = acc + cb_ref[...]
        mu = jnp.mean(y, axis=-1, keepdims=True)
        d = y - mu
        var = jnp.mean(d * d, axis=-1, keepdims=True)
        y = d * lax.rsqrt(var + LN_EPS) * lg_ref[...] + lb_ref[...]
        y = (y * jax.nn.sigmoid(y)).astype(BF16)
        a_ref[0, r0:r0 + rc, :] = jnp.dot(y, pw_ref[...], preferred_element_type=F32) + pb_ref[...]
    tail = pad_ref[tt:tt + CONV_HALO, :]
    cn_ref[0] = tail
    pad_ref[0:CONV_HALO, :] = tail


def _conv(zc, state, cw, cb, lg, lb, pw, pb, *, tt):
    b, t, _ = zc.shape
    return pl.pallas_call(
        functools.partial(_conv_kernel, tt=tt),
        grid=(b, t // tt),
        in_specs=[pl.BlockSpec((1, tt, 2 * CONV_W), lambda i, j: (i, j, 0)),
                  pl.BlockSpec((1, CONV_HALO, CONV_W), lambda i, j: (i, 0, 0)),
                  _const_spec((CONV_K, CONV_W)), _const_spec((1, CONV_W)), _const_spec((1, CONV_W)),
                  _const_spec((1, CONV_W)), _const_spec((CONV_W, CONV_W)), _const_spec((1, CONV_W))],
        out_specs=[pl.BlockSpec((1, tt, CONV_W), lambda i, j: (i, j, 0)),
                   pl.BlockSpec((1, CONV_HALO, CONV_W), lambda i, j: (i, 0, 0))],
        out_shape=[jax.ShapeDtypeStruct((b, t, CONV_W), F32),
                   jax.ShapeDtypeStruct((b, CONV_HALO, CONV_W), F32)],
        scratch_shapes=[pltpu.VMEM((CONV_HALO + tt, CONV_W), F32)],
        compiler_params=_params(("parallel", "arbitrary")),
        name="conv_module",
    )(zc, state, cw, cb, lg, lb, pw, pb)


def _lam(lam_ref, lam_init):
    lp = lam_ref[...]
    s1 = jnp.sum(lp[0:1] * lp[1:2], axis=-1, keepdims=True)
    s2 = jnp.sum(lp[2:3] * lp[3:4], axis=-1, keepdims=True)
    return jnp.exp(s1) - jnp.exp(s2) + lam_init


def _diff_combine(acc0, l0, acc1, l1, lam, g, lam_init):
    o = acc0 / l0 - lam * (acc1 / l1)
    return _rms(o, g) * (1.0 - lam_init)


def _attn_prompt_kernel(q_ref, k_ref, vt_ref, lam_ref, g_ref, o_ref, acc_ref, sa_ref, sb_ref, *, lam_init):
    qi = pl.program_id(1)
    tq = ATT_BLOCK
    qt = (q_ref[...] * LOG2E).T
    first = lax.broadcasted_iota(jnp.int32, qt.shape, 0) < ATT_DH
    qc = (jnp.where(first, qt, 0.0).astype(BF16), jnp.where(first, 0.0, qt).astype(BF16))
    acc_ref[...] = jnp.zeros(acc_ref.shape, F32)

    def logits(kc, s_ref):
        kj = k_ref[kc]
        for c in range(2):
            s_ref[c] = jnp.dot(kj, qc[c], preferred_element_type=F32)

    def update(kc, s_ref, carry, masked):
        vj = vt_ref[kc]
        out = []
        for c in range(2):
            m_old, l_old = carry[c]
            s = s_ref[c]
            if masked:
                key = lax.broadcasted_iota(jnp.int32, s.shape, 0) + (kc - qi) * tq
                qry = lax.broadcasted_iota(jnp.int32, s.shape, 1)
                s = jnp.where(key <= qry, s, NEG_INF)
            m_new = jnp.maximum(m_old, jnp.max(s, axis=0, keepdims=True))
            alpha = jnp.exp2(m_old - m_new)
            p = jnp.exp2(s - m_new)
            l_new = alpha * l_old + jnp.sum(p, axis=0, keepdims=True)
            acc_ref[c] = alpha * acc_ref[c] + jnp.dot(vj, p.astype(BF16), preferred_element_type=F32)
            out.append((m_new, l_new))
        return tuple(out)

    last = qi // 2

    def pair(j, carry):
        logits(2 * j + 1, sb_ref)
        carry = update(2 * j, sa_ref, carry, False)
        logits(2 * j + 2, sa_ref)
        return update(2 * j + 1, sb_ref, carry, False)

    logits(0, sa_ref)
    init = ((jnp.full((1, tq), NEG_INF, F32), jnp.zeros((1, tq), F32)),) * 2
    carry = lax.fori_loop(0, last, pair, init)
    logits(2 * last + 1, sb_ref)
    carry = update(2 * last, sa_ref, carry, True)
    (_, l0), (_, l1) = update(2 * last + 1, sb_ref, carry, True)
    o_t = acc_ref[0] / l0 - _lam(lam_ref, lam_init) * (acc_ref[1] / l1)
    o_ref[...] = _rms(o_t.T, g_ref[...]) * (1.0 - lam_init)


def _attn_prompt(q, kb, vtb, lam_p, g, *, lam_init):
    n = q.shape[0]
    nc = n // ATT_BLOCK
    return pl.pallas_call(
        functools.partial(_attn_prompt_kernel, lam_init=lam_init),
        grid=(ATT_HEADS, nc),
        in_specs=[pl.BlockSpec((ATT_BLOCK, HEAD_W), lambda h, i: (i, h)),
                  pl.BlockSpec((nc, ATT_BLOCK, HEAD_W), lambda h, i: (0, 0, h)),
                  pl.BlockSpec((nc, HEAD_W, ATT_BLOCK), lambda h, i: (0, h, 0)),
                  pl.BlockSpec((4, ATT_DH), lambda h, i: (0, 0)),
                  pl.BlockSpec((1, HEAD_W), lambda h, i: (0, 0))],
        out_specs=pl.BlockSpec((ATT_BLOCK, HEAD_W), lambda h, i: (i, h)),
        out_shape=jax.ShapeDtypeStruct((n, ATT_W), F32),
        scratch_shapes=[pltpu.VMEM((2, HEAD_W, ATT_BLOCK), F32)]
                       + [pltpu.VMEM((2, ATT_BLOCK, ATT_BLOCK), F32)] * 2,
        compiler_params=_params(("parallel", "arbitrary")),
        name="attn_prompt",
    )(q, kb, vtb, lam_p, g)


def _attn_sample_kernel(pt_ref, q_ref, kn_ref, vn_ref, *rest, lam_init, t_new):
    del pt_ref
    k_refs = rest[:PAGES_PER_STEP]
    v_refs = rest[PAGES_PER_STEP:2 * PAGES_PER_STEP]
    lam_ref, g_ref, o_ref, m_ref, l_ref, acc_ref = rest[2 * PAGES_PER_STEP:]
    step = pl.program_id(1)
    rows = ATT_HEADS * 2 * t_new
    page_cols = PAGE_SIZE * ATT_HEADS

    q = q_ref[...]
    first = lax.broadcasted_iota(jnp.int32, (t_new, HEAD_W), 1) < ATT_DH
    pieces = []
    for h in range(ATT_HEADS):
        qh = q[:, h * HEAD_W:(h + 1) * HEAD_W]
        pieces += [jnp.where(first, qh, 0.0), jnp.where(first, 0.0, qh)]
    qm = jnp.concatenate(pieces, axis=0).astype(BF16)

    def update(s, v):
        m_old = m_ref[...]
        m_new = jnp.maximum(m_old, jnp.max(s, axis=-1, keepdims=True))
        alpha = jnp.exp(m_old - m_new)
        p = jnp.exp(s - m_new)
        l_ref[...] = alpha * l_ref[...] + jnp.sum(p, axis=-1, keepdims=True)
        acc_ref[...] = alpha * acc_ref[...] + jnp.dot(p.astype(BF16), v, preferred_element_type=F32)
        m_ref[...] = m_new

    def logits(keys):
        return lax.dot_general(qm, keys, (((1,), (1,)), ((), ())), preferred_element_type=F32)

    @pl.when(step == 0)
    def _():
        m_ref[...] = jnp.full(m_ref.shape, NEG_INF, F32)
        l_ref[...] = jnp.zeros(l_ref.shape, F32)
        acc_ref[...] = jnp.zeros(acc_ref.shape, F32)
        zpad = jnp.zeros((PAGE_SIZE - ATT_HEADS * t_new, HEAD_W), F32)
        stack = lambda ref: jnp.concatenate(
            [ref[:, h * HEAD_W:(h + 1) * HEAD_W] for h in range(ATT_HEADS)] + [zpad], axis=0).astype(BF16)
        col = lax.broadcasted_iota(jnp.int32, (rows, PAGE_SIZE), 1)
        row = lax.broadcasted_iota(jnp.int32, (rows, PAGE_SIZE), 0)
        valid = (col // t_new == row // (2 * t_new)) & (col % t_new <= row % t_new)
        update(jnp.where(valid, logits(stack(kn_ref)), NEG_INF), stack(vn_ref))

    col = lax.broadcasted_iota(jnp.int32, (rows, page_cols), 1)
    row = lax.broadcasted_iota(jnp.int32, (rows, page_cols), 0)
    bias = jnp.where(col % ATT_HEADS == row // (2 * t_new), 0.0, NEG_INF)
    bias = jnp.concatenate([bias] * PAGES_PER_STEP, axis=1)
    keys = jnp.concatenate([r[...] for r in k_refs], axis=0).astype(BF16)
    vals = jnp.concatenate([r[...] for r in v_refs], axis=0).astype(BF16)
    update(logits(keys) + bias, vals)

    @pl.when(step == pl.num_programs(1) - 1)
    def _():
        lam = _lam(lam_ref, lam_init)
        acc = acc_ref[...]
        l = l_ref[...]
        outs = []
        for h in range(ATT_HEADS):
            r0, r1, r2 = 2 * h * t_new, (2 * h + 1) * t_new, (2 * h + 2) * t_new
            outs.append(_diff_combine(acc[r0:r1], l[r0:r1], acc[r1:r2], l[r1:r2], lam, g_ref[...], lam_init))
        o_ref[...] = jnp.concatenate(outs, axis=-1)


def _attn_sample(q, k_new, v_new, cache_k, cache_v, page_table, lam_p, g, *, layer, lam_init, t_new):
    n = q.shape[0]
    b = n // t_new
    n_pages = page_table.shape[1]
    steps = n_pages // PAGES_PER_STEP
    pool = cache_k.shape[1]
    ck = cache_k.reshape(DEPTH, pool, PAGE_SIZE * ATT_HEADS, HEAD_W)
    cv = cache_v.reshape(DEPTH, pool, PAGE_SIZE * ATT_HEADS, HEAD_W)
    row = pl.BlockSpec((t_new, ATT_W), lambda i, s, pt: (i, 0))

    def page_spec(j):
        return pl.BlockSpec((None, None, PAGE_SIZE * ATT_HEADS, HEAD_W),
                            lambda i, s, pt: (layer, pt[i, s * PAGES_PER_STEP + j], 0, 0))

    pages = [page_spec(j) for j in range(PAGES_PER_STEP)]
    rows = ATT_HEADS * 2 * t_new
    grid_spec = pltpu.PrefetchScalarGridSpec(
        num_scalar_prefetch=1,
        grid=(b, steps),
        in_specs=[row, row, row] + pages + pages + [
            pl.BlockSpec((4, ATT_DH), lambda i, s, pt: (0, 0)),
            pl.BlockSpec((1, HEAD_W), lambda i, s, pt: (0, 0))],
        out_specs=row,
        scratch_shapes=[pltpu.VMEM((rows, 1), F32), pltpu.VMEM((rows, 1), F32),
                        pltpu.VMEM((rows, HEAD_W), F32)],
    )
    return pl.pallas_call(
        functools.partial(_attn_sample_kernel, lam_init=lam_init, t_new=t_new),
        grid_spec=grid_spec,
        out_shape=jax.ShapeDtypeStruct((n, ATT_W), F32),
        compiler_params=_params(("parallel", "arbitrary")),
        name="attn_sample",
    )(page_table, q, k_new, v_new, *([ck] * PAGES_PER_STEP), *([cv] * PAGES_PER_STEP), lam_p, g)


def _rwkv_kernel(z_ref, sh_ref, s0_ref, mu_ref, w0_ref, w2_ref, a0_ref, a2_ref, g2_ref,
                 kk_ref, ka_ref, rk_ref, lng_ref, lnb_ref, seg_ref,
                 c_ref, shn_ref, sn_ref,
                 carry_ref, st_ref, r_s, w_s, k_s, kk_s, b_s, vt_s, yt_s, *, n_valid):
    tt = SCAN_TILE

    @pl.when(pl.program_id(1) == 0)
    def _():
        carry_ref[...] = sh_ref[0]
        st_ref[...] = s0_ref[0]

    z = z_ref[0]
    rowi = lax.broadcasted_iota(jnp.int32, z.shape, 0)
    prev = jnp.where(rowi == 0, carry_ref[...], pltpu.roll(z, 1, 0))
    shn_ref[0] = z[n_valid - 1:n_valid, :]
    carry_ref[...] = z[tt - 1:tt, :]
    zs = z + (prev - z) * mu_ref[...]

    def segsum(x):
        return jnp.dot(x, seg_ref[...], precision=HI, preferred_element_type=F32)

    r = zs[:, 0:RWKV_W]
    k = zs[:, RWKV_W:2 * RWKV_W]
    v = zs[:, 2 * RWKV_W:3 * RWKV_W]
    wa = zs[:, 3 * RWKV_W:3 * RWKV_W + W_LORA + A_LORA]
    gd = zs[:, 3 * RWKV_W + W_LORA + A_LORA:]
    wx = w0_ref[...] + jnp.dot(jnp.tanh(wa), w2_ref[...], precision=HI, preferred_element_type=F32)
    softplus = jnp.maximum(-wx, 0.0) + jnp.log1p(jnp.exp(-jnp.abs(wx)))
    w_log = -softplus - 0.5
    decay = jnp.exp(-jnp.exp(w_log))
    a = jax.nn.sigmoid(a0_ref[...] + jnp.dot(wa, a2_ref[...], precision=HI, preferred_element_type=F32))
    g = jnp.dot(jax.nn.sigmoid(gd), g2_ref[...], precision=HI, preferred_element_type=F32)
    kk = k * kk_ref[...]
    kk = kk * lax.rsqrt(segsum(kk * kk) + 1e-12)
    k = k * (1.0 + (a - 1.0) * ka_ref[...])
    bonus = segsum(r * k * rk_ref[...]) * v

    r_s[...] = r
    w_s[...] = decay
    k_s[...] = k
    kk_s[...] = kk
    b_s[...] = kk * a
    vt_s[...] = v.T
    yt_s[...] = jnp.zeros(yt_s.shape, F32)

    lane = lax.broadcasted_iota(jnp.int32, (RWKV_DH, 2 * RWKV_DH), 1)
    lo = lane < RWKV_DH

    def pair_sum(x):
        s_lo = jnp.sum(jnp.where(lo, x, 0.0), axis=-1, keepdims=True)
        s_hi = jnp.sum(jnp.where(lo, 0.0, x), axis=-1, keepdims=True)
        return s_lo, s_hi

    def step(t, rows8, i, states):
        hot = lane == t
        new_states = []
        for p in range(2):
            sl = slice(p * 2 * RWKV_DH, (p + 1) * 2 * RWKV_DH)
            row_t = lambda name: rows8[name][i:i + 1, sl]
            s_old = states[p]
            sa_lo, sa_hi = pair_sum(s_old * row_t("kk"))
            h0 = slice(2 * p * RWKV_DH, (2 * p + 1) * RWKV_DH)
            h1 = slice((2 * p + 1) * RWKV_DH, (2 * p + 2) * RWKV_DH)
            v_lo = jnp.sum(jnp.where(hot, vt_s[h0, :], 0.0), axis=-1, keepdims=True)
            v_hi = jnp.sum(jnp.where(hot, vt_s[h1, :], 0.0), axis=-1, keepdims=True)
            s_new = (s_old * row_t("w") - jnp.where(lo, sa_lo, sa_hi) * row_t("b")
                     + jnp.where(lo, v_lo, v_hi) * row_t("k"))
            y_lo, y_hi = pair_sum(s_new * row_t("r"))
            yt_s[h0, :] = jnp.where(hot, y_lo, yt_s[h0, :])
            yt_s[h1, :] = jnp.where(hot, y_hi, yt_s[h1, :])
            new_states.append(s_new)
        return tuple(new_states)

    sub = 8

    def group(gi, states):
        base = pl.multiple_of(gi * sub, sub)
        rows8 = {name: ref[pl.ds(base, sub), :]
                 for name, ref in (("r", r_s), ("w", w_s), ("k", k_s), ("kk", kk_s), ("b", b_s))}
        for i in range(sub):
            states = step(base + i, rows8, i, states)
        return states

    final = lax.fori_loop(0, n_valid // sub, group, (st_ref[0], st_ref[1]))
    st_ref[0] = final[0]
    st_ref[1] = final[1]
    sn_ref[0] = st_ref[...]

    y = yt_s[...].T
    mean = segsum(y) * (1.0 / RWKV_DH)
    d = y - mean
    var = segsum(d * d) * (1.0 / RWKV_DH)
    y = d * lax.rsqrt(var + RWKV_GN_EPS) * lng_ref[...] + lnb_ref[...]
    c_ref[0] = (y + bonus) * g


def _rwkv(z, shift, state, mu, w0, w2p, a0, a2p, g2, kk, ka, rk, lng, lnb, seg, *, n_valid):
    b, t, _ = z.shape
    tt = SCAN_TILE
    vec = lambda w: _const_spec((1, w))
    return pl.pallas_call(
        functools.partial(_rwkv_kernel, n_valid=n_valid),
        grid=(b, t // tt),
        in_specs=[pl.BlockSpec((1, tt, RWKV_IN), lambda i, j: (i, j, 0)),
                  pl.BlockSpec((1, 1, RWKV_IN), lambda i, j: (i, 0, 0)),
                  pl.BlockSpec((1, 2, RWKV_DH, 2 * RWKV_DH), lambda i, j: (i, 0, 0, 0)),
                  vec(RWKV_IN), vec(RWKV_W), _const_spec((W_LORA + A_LORA, RWKV_W)), vec(RWKV_W),
                  _const_spec((W_LORA + A_LORA, RWKV_W)), _const_spec((G_LORA, RWKV_W)),
                  vec(RWKV_W), vec(RWKV_W), vec(RWKV_W), vec(RWKV_W), vec(RWKV_W),
                  _const_spec((RWKV_W, RWKV_W))],
        out_specs=[pl.BlockSpec((1, tt, RWKV_W), lambda i, j: (i, j, 0)),
                   pl.BlockSpec((1, 1, RWKV_IN), lambda i, j: (i, 0, 0)),
                   pl.BlockSpec((1, 2, RWKV_DH, 2 * RWKV_DH), lambda i, j: (i, 0, 0, 0))],
        out_shape=[jax.ShapeDtypeStruct((b, t, RWKV_W), F32),
                   jax.ShapeDtypeStruct((b, 1, RWKV_IN), F32),
                   jax.ShapeDtypeStruct((b, 2, RWKV_DH, 2 * RWKV_DH), F32)],
        scratch_shapes=[pltpu.VMEM((1, RWKV_IN), F32), pltpu.VMEM((2, RWKV_DH, 2 * RWKV_DH), F32)]
                       + [pltpu.VMEM((tt, RWKV_W), F32)] * 5
                       + [pltpu.VMEM((RWKV_W, tt), F32)] * 2,
        compiler_params=_params(("parallel", "arbitrary")),
        name="rwkv7",
    )(z, shift, state, mu, w0, w2p, a0, a2p, g2, kk, ka, rk, lng, lnb, seg)


RW_CHUNK = 64
RW_TILE = 256
SUBLANES = 8
NN = (((1,), (0,)), ((), ()))
NT = (((1,), (1,)), ((), ()))


def _split(a):
    hi = a.astype(BF16)
    return hi, (a - hi.astype(F32)).astype(BF16)


def _dot3s(a, b, dims=NN):
    d = lambda x, y: lax.dot_general(x, y, dims, preferred_element_type=F32)
    return d(a[0], b[0]) + (d(a[0], b[1]) + d(a[1], b[0]))


def _dot3(a, b, dims=NN):
    return _dot3s(_split(a), _split(b), dims)


def _rwkv_chunk_kernel(z_ref, sh_ref, s0_ref, mu_ref, w0_ref, w2_ref, a0_ref, a2_ref, g2_ref,
                       kk_ref, ka_ref, rk_ref, lng_ref, lnb_ref, seg_ref,
                       c_ref, shn_ref, sn_ref, carry_ref, st_ref, y_s):
    tt, cs = RW_TILE, RW_CHUNK

    @pl.when(pl.program_id(1) == 0)
    def _():
        carry_ref[...] = sh_ref[0]
        st_ref[...] = s0_ref[0]

    z = z_ref[0]
    rowi = lax.broadcasted_iota(jnp.int32, z.shape, 0)
    prev = jnp.where(rowi == 0, carry_ref[...], pltpu.roll(z, 1, 0))
    shn_ref[0] = z[tt - 1:tt, :]
    carry_ref[...] = z[tt - 1:tt, :]
    zs = z + (prev - z) * mu_ref[...]

    seg = _split(seg_ref[...])

    def segsum(x):
        return _dot3s(_split(x), seg)

    r = zs[:, 0:RWKV_W]
    k = zs[:, RWKV_W:2 * RWKV_W]
    v = zs[:, 2 * RWKV_W:3 * RWKV_W]
    wa = zs[:, 3 * RWKV_W:3 * RWKV_W + W_LORA + A_LORA]
    gd = zs[:, 3 * RWKV_W + W_LORA + A_LORA:]
    wx = w0_ref[...] + _dot3(jnp.tanh(wa), w2_ref[...])
    softplus = jnp.maximum(-wx, 0.0) + jnp.log1p(jnp.exp(-jnp.abs(wx)))
    logw = -jnp.exp(-softplus - 0.5)
    a = jax.nn.sigmoid(a0_ref[...] + _dot3(wa, a2_ref[...]))
    g = _dot3(jax.nn.sigmoid(gd), g2_ref[...])
    kk = k * kk_ref[...]
    kk = kk * lax.rsqrt(segsum(kk * kk) + 1e-12)
    k = k * (1.0 + (a - 1.0) * ka_ref[...])
    bonus = segsum(r * k * rk_ref[...]) * v
    b = kk * a

    lane_head = lax.broadcasted_iota(jnp.int32, (1, RWKV_W), 1) // RWKV_DH
    head_mask = [lane_head == h for h in range(RWKV_HEADS)]
    row = lax.broadcasted_iota(jnp.int32, (2 * cs, cs), 0)
    col = lax.broadcasted_iota(jnp.int32, (2 * cs, cs), 1)
    tril2 = col <= jnp.where(row < cs, row - 1, row - cs)
    tri_incl = _split((lax.broadcasted_iota(jnp.int32, (cs, cs), 0)
                       >= lax.broadcasted_iota(jnp.int32, (cs, cs), 1)).astype(F32))
    bd_row = lax.broadcasted_iota(jnp.int32, (RWKV_W, RWKV_W), 0) // RWKV_DH
    bd_col = lax.broadcasted_iota(jnp.int32, (RWKV_W, RWKV_W), 1) // RWKV_DH
    block_diag = bd_row == bd_col
    lo_half = lax.broadcasted_iota(jnp.int32, (SUBLANES, 2 * RWKV_DH), 1) < RWKV_DH
    n_sub = cs // SUBLANES

    for ci in range(tt // cs):
        sl = slice(ci * cs, (ci + 1) * cs)
        r_c, lw_c, k_c, v_c, kk_c, b_c = r[sl], logw[sl], k[sl], v[sl], kk[sl], b[sl]
        cum = _dot3s(tri_incl, _split(lw_c))
        total = cum[cs - 1:cs, :]
        inv = jnp.exp(-cum)
        fin = jnp.exp(total - cum)
        lhs = _split(jnp.concatenate([kk_c * jnp.exp(cum - lw_c), r_c * jnp.exp(cum)], axis=0))
        kh, bh = _split(k_c * inv), _split(b_c * inv)
        v_s = _split(v_c)

        wy = jnp.zeros((2 * cs, RWKV_W), F32)
        l_mats, yb_mats = [], []
        for h in range(RWKV_HEADS):
            lhs_h = (jnp.where(head_mask[h], lhs[0], 0), jnp.where(head_mask[h], lhs[1], 0))
            gk = jnp.where(tril2, _dot3s(lhs_h, kh, NT), 0.0)
            gb = jnp.where(tril2, _dot3s(lhs_h, bh, NT), 0.0)
            wy = wy + jnp.where(head_mask[h], _dot3s(_split(gk), v_s), 0.0)
            l_mats.append(gb[:cs])
            yb_mats.append(gb[cs:])

        state = st_ref[...]
        sm = _dot3s(lhs, _split(state), NT)
        rhs = sm[:cs] + wy[:cs]
        tiles = [rhs[i * SUBLANES:(i + 1) * SUBLANES] for i in range(n_sub)]
        for j in range(cs):
            i0 = j // SUBLANES
            u_j = tiles[i0][j % SUBLANES:j % SUBLANES + 1, :]
            for i in range(i0, n_sub):
                rows = slice(i * SUBLANES, (i + 1) * SUBLANES)
                bc = lambda h: jnp.broadcast_to(l_mats[h][rows, j:j + 1], (SUBLANES, 2 * RWKV_DH))
                l_col = jnp.concatenate([jnp.where(lo_half, bc(0), bc(1)),
                                         jnp.where(lo_half, bc(2), bc(3))], axis=1)
                tiles[i] = tiles[i] - l_col * u_j
        u = jnp.concatenate(tiles, axis=0)
        u_s = _split(u)
        bu = jnp.zeros((cs, RWKV_W), F32)
        for h in range(RWKV_HEADS):
            bu = bu + jnp.where(head_mask[h], _dot3s(_split(yb_mats[h]), u_s), 0.0)
        y_s[sl, :] = sm[cs:] + wy[cs:] - bu
        delta = _dot3s(_split(v_c.T), _split(k_c * fin)) - _dot3s(_split(u.T), _split(b_c * fin))
        st_ref[...] = state * jnp.exp(total) + jnp.where(block_diag, delta, 0.0)

    sn_ref[0] = st_ref[...]
    y = y_s[...]
    mean = segsum(y) * (1.0 / RWKV_DH)
    d = y - mean
    var = segsum(d * d) * (1.0 / RWKV_DH)
    y = d * lax.rsqrt(var + RWKV_GN_EPS) * lng_ref[...] + lnb_ref[...]
    c_ref[0] = (y + bonus) * g


def _rwkv_chunked(z, shift, state, mu, w0, w2p, a0, a2p, g2, kk, ka, rk, lng, lnb, seg):
    b, t, _ = z.shape
    tt = RW_TILE
    vec = lambda w: _const_spec((1, w))
    return pl.pallas_call(
        _rwkv_chunk_kernel,
        grid=(b, t // tt),
        in_specs=[pl.BlockSpec((1, tt, RWKV_IN), lambda i, j: (i, j, 0)),
                  pl.BlockSpec((1, 1, RWKV_IN), lambda i, j: (i, 0, 0)),
                  pl.BlockSpec((1, RWKV_W, RWKV_W), lambda i, j: (i, 0, 0)),
                  vec(RWKV_IN), vec(RWKV_W), _const_spec((W_LORA + A_LORA, RWKV_W)), vec(RWKV_W),
                  _const_spec((W_LORA + A_LORA, RWKV_W)), _const_spec((G_LORA, RWKV_W)),
                  vec(RWKV_W), vec(RWKV_W), vec(RWKV_W), vec(RWKV_W), vec(RWKV_W),
                  _const_spec((RWKV_W, RWKV_W))],
        out_specs=[pl.BlockSpec((1, tt, RWKV_W), lambda i, j: (i, j, 0)),
                   pl.BlockSpec((1, 1, RWKV_IN), lambda i, j: (i, 0, 0)),
                   pl.BlockSpec((1, RWKV_W, RWKV_W), lambda i, j: (i, 0, 0))],
        out_shape=[jax.ShapeDtypeStruct((b, t, RWKV_W), F32),
                   jax.ShapeDtypeStruct((b, 1, RWKV_IN), F32),
                   jax.ShapeDtypeStruct((b, RWKV_W, RWKV_W), F32)],
        scratch_shapes=[pltpu.VMEM((1, RWKV_IN), F32), pltpu.VMEM((RWKV_W, RWKV_W), F32),
                        pltpu.VMEM((tt, RWKV_W), F32)],
        compiler_params=_params(("parallel", "arbitrary")),
        name="rwkv7_chunked",
    )(z, shift, state, mu, w0, w2p, a0, a2p, g2, kk, ka, rk, lng, lnb, seg)


def _block_diag_state(s):
    b = s.shape[0]
    eye = jnp.eye(RWKV_HEADS, dtype=s.dtype)
    return jnp.einsum("bhvk,hg->bhvgk", s, eye).reshape(b, RWKV_W, RWKV_W)


def _diag_blocks(s):
    b = s.shape[0]
    s = s.reshape(b, RWKV_HEADS, RWKV_DH, RWKV_HEADS, RWKV_DH)
    return jnp.stack([s[:, h, :, h, :] for h in range(RWKV_HEADS)], axis=1)


def _pair_state(s):
    b = s.shape[0]
    s = s.reshape(b, 2, 2, RWKV_DH, RWKV_DH)
    return jnp.transpose(s, (0, 1, 3, 2, 4)).reshape(b, 2, RWKV_DH, 2 * RWKV_DH)


def _unpair_state(s):
    b = s.shape[0]
    s = s.reshape(b, 2, RWKV_DH, 2, RWKV_DH)
    return jnp.transpose(s, (0, 1, 3, 2, 4)).reshape(b, RWKV_HEADS, RWKV_DH, RWKV_DH)


def _out_ffn_kernel(x_ref, a_ref, b_ref, c_ref, gs_ref, wo_ref, g2_ref, wg_ref, wu_ref, wd_ref,
                    fg_ref, o_ref, acc_ref, *, final):
    mix = jnp.concatenate([a_ref[...], b_ref[...], c_ref[...]], axis=-1) * gs_ref[...]
    x1 = x_ref[...] + jnp.dot(mix.astype(BF16), wo_ref[...], preferred_element_type=F32)
    acc_ref[...] = x1
    h2 = _rms(x1, g2_ref[...]).astype(BF16)
    for c in range(0, D_FF, FF_CHUNK):
        gate = jnp.dot(h2, wg_ref[:, c:c + FF_CHUNK], preferred_element_type=F32)
        up = jnp.dot(h2, wu_ref[:, c:c + FF_CHUNK], preferred_element_type=F32)
        act = (gate * jax.nn.sigmoid(gate) * up).astype(BF16)
        acc_ref[...] += jnp.dot(act, wd_ref[c:c + FF_CHUNK, :], preferred_element_type=F32)
    x2 = acc_ref[...]
    o_ref[...] = _rms(x2, fg_ref[...]) if final else x2


def _out_ffn(x, a, b, c, gs, wo, g2, wg, wu, wd, fg, *, tm, final):
    n = x.shape[0]
    row = lambda w: pl.BlockSpec((tm, w), lambda i: (i, 0))
    return pl.pallas_call(
        functools.partial(_out_ffn_kernel, final=final),
        grid=(n // tm,),
        in_specs=[row(D_MODEL), row(CONV_W), row(ATT_W), row(RWKV_W), _const_spec((1, D_MODEL)),
                  _const_spec((D_MODEL, D_MODEL)), _const_spec((1, D_MODEL)),
                  _const_spec((D_MODEL, D_FF)), _const_spec((D_MODEL, D_FF)),
                  _const_spec((D_FF, D_MODEL)), _const_spec((1, D_MODEL))],
        out_specs=row(D_MODEL),
        out_shape=jax.ShapeDtypeStruct((n, D_MODEL), F32),
        scratch_shapes=[pltpu.VMEM((tm, D_MODEL), F32)],
        compiler_params=_params(("parallel",)),
        name="out_ffn",
    )(x, a, b, c, gs, wo, g2, wg, wu, wd, fg)


def _rope_tables(pos):
    half = ATT_DH // 2
    inv = ROPE_THETA ** (-jnp.arange(half, dtype=F32) / half)
    ang = pos.astype(F32)[:, None] * inv[None, :]
    tile = lambda t: jnp.concatenate([t] * (HEAD_W // half), axis=-1)
    return tile(jnp.cos(ang)), tile(jnp.sin(ang))


def _rotate_half_columns(w):
    half = ATT_DH // 2
    w = w.reshape(w.shape[0], ATT_W // ATT_DH, 2, half)
    return jnp.concatenate([-w[:, :, 1], w[:, :, 0]], axis=-1).reshape(w.shape[0], ATT_W)


def kernel(x_prompt, x_sample, cache_k, cache_v, state_conv, state_shift, state_wkv, page_table, norm1_g, w_in, conv_w, conv_b, conv_ln_g, conv_ln_b, conv_pw_w, conv_pw_b, att_lambda, att_subln_g, rwkv_mu, rwkv_w0, rwkv_w2, rwkv_a0, rwkv_a2, rwkv_g2, rwkv_kk, rwkv_ka, rwkv_rk, rwkv_ln_g, rwkv_ln_b, group_scale, w_out, norm2_g, w_gate, w_up, w_down, final_g):
    bp, tp, _ = x_prompt.shape
    bs, ts, _ = x_sample.shape
    assert bp == 1 and tp % ATT_BLOCK == 0 and ts <= SCAN_TILE and ts % 8 == 0
    past_len = page_table.shape[1] * PAGE_SIZE
    groups = {
        "p": dict(b=bp, t=tp, tm=512, conv_tt=256, x=x_prompt.reshape(bp * tp, D_MODEL),
                  pos=jnp.arange(tp, dtype=jnp.int32)),
        "s": dict(b=bs, t=ts, tm=bs * ts, conv_tt=ts, x=x_sample.reshape(bs * ts, D_MODEL),
                  pos=jnp.tile(past_len + jnp.arange(ts, dtype=jnp.int32), bs)),
    }
    for grp in groups.values():
        grp["cos"], grp["sin"] = _rope_tables(grp["pos"])
    outs = {name: dict(k=[], v=[], conv=[], shift=[], wkv=[]) for name in groups}

    seg = jnp.asarray(np.kron(np.eye(RWKV_HEADS, dtype=np.float32),
                              np.ones((RWKV_DH, RWKV_DH), np.float32)))
    row = lambda p: p.reshape(1, -1)

    for l in range(DEPTH):
        lam_init = 0.8 - 0.6 * math.exp(-0.3 * l)
        att_w = w_in[l][:, 2 * CONV_W:2 * CONV_W + 2 * ATT_W]
        w_ext = jnp.concatenate([w_in[l], _rotate_half_columns(att_w[:, :ATT_W]),
                                 _rotate_half_columns(att_w[:, ATT_W:])], axis=-1).astype(BF16)
        w2p = jnp.concatenate([rwkv_w2[l], jnp.zeros((A_LORA, RWKV_W), F32)], axis=0)
        a2p = jnp.concatenate([jnp.zeros((W_LORA, RWKV_W), F32), rwkv_a2[l]], axis=0)
        subln = row(att_subln_g[l])
        for name, grp in groups.items():
            b, t = grp["b"], grp["t"]
            prompt = name == "p"
            zc, q, k, v, zr, *kv_bf = _in_proj(grp["x"], row(norm1_g[l]), w_ext, grp["cos"], grp["sin"],
                                               tm=grp["tm"], emit_bf16=prompt)
            if prompt:
                conv_state = jnp.zeros((b, CONV_HALO, CONV_W), F32)
            else:
                conv_state = jnp.pad(state_conv[l], ((0, 0), (CONV_HALO - (CONV_K - 1), 0), (0, 0)))
            a_out, conv_new = _conv(zc.reshape(b, t, 2 * CONV_W), conv_state, conv_w[l].reshape(CONV_K, CONV_W),
                                    row(conv_b[l]), row(conv_ln_g[l]), row(conv_ln_b[l]),
                                    conv_pw_w[l].astype(BF16), row(conv_pw_b[l]), tt=grp["conv_tt"])
            if prompt:
                b_out = _attn_prompt(q, kv_bf[0], kv_bf[1], att_lambda[l], subln, lam_init=lam_init)
            else:
                b_out = _attn_sample(q, k, v, cache_k, cache_v, page_table, att_lambda[l], subln,
                                     layer=l, lam_init=lam_init, t_new=t)
            zr3 = zr.reshape(b, t, RWKV_IN)
            rw_params = (row(rwkv_mu[l]), row(rwkv_w0[l]), w2p, row(rwkv_a0[l]), a2p, rwkv_g2[l],
                         row(rwkv_kk[l]), row(rwkv_ka[l]), row(rwkv_rk[l]), row(rwkv_ln_g[l]),
                         row(rwkv_ln_b[l]), seg)
            if prompt:
                c_out, shift_new, wkv_new = _rwkv_chunked(
                    zr3, jnp.zeros((b, 1, RWKV_IN), F32), jnp.zeros((b, RWKV_W, RWKV_W), F32), *rw_params)
                wkv_new = _diag_blocks(wkv_new)
            else:
                zr3 = jnp.pad(zr3, ((0, 0), (0, SCAN_TILE - t), (0, 0)))
                c_out, shift_new, wkv_new = _rwkv(
                    zr3, state_shift[l].reshape(b, 1, RWKV_IN), _pair_state(state_wkv[l]), *rw_params,
                    n_valid=t)
                wkv_new = _unpair_state(wkv_new)
            c_out = c_out[:, :t].reshape(b * t, RWKV_W)
            grp["x"] = _out_ffn(grp["x"], a_out.reshape(b * t, CONV_W), b_out, c_out, row(group_scale[l]),
                                w_out[l].astype(BF16), row(norm2_g[l]), w_gate[l].astype(BF16),
                                w_up[l].astype(BF16), w_down[l].astype(BF16), row(final_g),
                                tm=grp["tm"], final=(l == DEPTH - 1))
            o = outs[name]
            o["k"].append(k.reshape(b, t, ATT_HEADS, HEAD_W))
            o["v"].append(v.reshape(b, t, ATT_HEADS, HEAD_W))
            o["conv"].append(conv_new[:, CONV_HALO - (CONV_K - 1):])
            o["shift"].append(shift_new.reshape(b, RWKV_IN))
            o["wkv"].append(wkv_new)

    y_prompt = groups["p"]["x"].reshape(bp, tp, D_MODEL)
    y_sample = groups["s"]["x"].reshape(bs, ts, D_MODEL)
    st = lambda name, key: jnp.stack(outs[name][key])
    return (y_prompt, y_sample,
            st("p", "k"), st("p", "v"), st("p", "conv"), st("p", "shift"), st("p", "wkv"),
            st("s", "k"), st("s", "v"), st("s", "conv"), st("s", "shift"), st("s", "wkv"))
```

```python
import functools
import math

import jax
import jax.numpy as jnp
import numpy as np
from jax import lax
from jax.experimental import pallas as pl
from jax.experimental.pallas import tpu as pltpu

D_MODEL = 1024
DEPTH = 2
PAGE_SIZE = 128
CONV_W = 256
ATT_W = 512
RWKV_W = 256
ATT_HEADS = 4
ATT_DH = 64
HEAD_W = 2 * ATT_DH
RWKV_DH = 64
RWKV_HEADS = 4
CONV_K = 31
W_LORA = 64
A_LORA = 64
G_LORA = 128
RWKV_IN = 3 * RWKV_W + W_LORA + A_LORA + G_LORA
IN_W = 2 * CONV_W + 3 * ATT_W + RWKV_IN
D_FF = 2816
ROPE_THETA = 10000.0
NORM_EPS = 1e-6
LN_EPS = 1e-5
RWKV_GN_EPS = 64e-5
NEG_INF = -1e30

F32 = jnp.float32
BF16 = jnp.bfloat16
HI = lax.Precision.HIGHEST

CONV_HALO = 32
CONV_ROWS = 64
FF_CHUNK = 256
SCAN_TILE = 128
ATT_BLOCK = 512
ATT_QBLOCK = 2 * ATT_BLOCK
LOG2E = math.log2(math.e)
PAGES_PER_STEP = 16
VMEM_LIMIT = 56 * 1024 * 1024


def _const_spec(shape):
    zeros = (0,) * len(shape)
    return pl.BlockSpec(shape, lambda *_: zeros, pipeline_mode=pl.Buffered(1))


def _params(sem):
    return pltpu.CompilerParams(dimension_semantics=sem, vmem_limit_bytes=VMEM_LIMIT)


def _rms(x, g):
    return x * lax.rsqrt(jnp.mean(x * x, axis=-1, keepdims=True) + NORM_EPS) * g


def _in_proj_kernel(x_ref, g_ref, w_ref, cos_ref, sin_ref, *out_refs, emit_bf16):
    if emit_bf16:
        zc_ref, q_ref, k_ref, v_ref, zr_ref, kb_ref, vb_ref = out_refs
    else:
        zc_ref, q_ref, k_ref, v_ref, zr_ref = out_refs
    h = _rms(x_ref[...], g_ref[...]).astype(BF16)

    def mm(c0, c1):
        return jnp.dot(h, w_ref[:, c0:c1], preferred_element_type=F32)

    cos = jnp.concatenate([cos_ref[...]] * ATT_HEADS, axis=-1)
    sin = jnp.concatenate([sin_ref[...]] * ATT_HEADS, axis=-1)
    q0, k0, v0, r0 = 2 * CONV_W, 2 * CONV_W + ATT_W, 2 * CONV_W + 2 * ATT_W, 2 * CONV_W + 3 * ATT_W
    zc_ref[...] = mm(0, q0)
    q = mm(q0, k0) * cos + mm(IN_W, IN_W + ATT_W) * sin
    q_ref[...] = q * (ATT_DH ** -0.5)
    k = mm(k0, v0) * cos + mm(IN_W + ATT_W, IN_W + 2 * ATT_W) * sin
    k_ref[...] = k
    v = mm(v0, r0)
    v_ref[...] = v
    zr_ref[...] = mm(r0, IN_W)
    if emit_bf16:
        kb_ref[0] = k.astype(BF16)
        vb_ref[0] = v.T.astype(BF16)


def _in_proj(x, g, w_ext, cos, sin, *, tm, emit_bf16):
    n = x.shape[0]
    row = lambda w: pl.BlockSpec((tm, w), lambda i: (i, 0))
    widths = [2 * CONV_W, ATT_W, ATT_W, ATT_W, RWKV_IN]
    out_shape = [jax.ShapeDtypeStruct((n, w), F32) for w in widths]
    out_specs = [row(w) for w in widths]
    if emit_bf16:
        assert tm == ATT_BLOCK
        out_shape += [jax.ShapeDtypeStruct((n // tm, tm, ATT_W), BF16),
                      jax.ShapeDtypeStruct((n // tm, ATT_W, tm), BF16)]
        out_specs += [pl.BlockSpec((1, tm, ATT_W), lambda i: (i, 0, 0)),
                      pl.BlockSpec((1, ATT_W, tm), lambda i: (i, 0, 0))]
    return pl.pallas_call(
        functools.partial(_in_proj_kernel, emit_bf16=emit_bf16),
        grid=(n // tm,),
        in_specs=[row(D_MODEL), _const_spec((1, D_MODEL)), _const_spec(w_ext.shape),
                  row(HEAD_W), row(HEAD_W)],
        out_specs=out_specs,
        out_shape=out_shape,
        compiler_params=_params(("parallel",)),
        name="in_proj",
    )(x, g, w_ext, cos, sin)


def _conv_kernel(zc_ref, st_ref, cw_ref, cb_ref, lg_ref, lb_ref, pw_ref, pb_ref,
                 a_ref, cn_ref, pad_ref, *, tt):
    @pl.when(pl.program_id(1) == 0)
    def _():
        pad_ref[0:CONV_HALO, :] = st_ref[0]

    z = zc_ref[0]
    pad_ref[CONV_HALO:CONV_HALO + tt, :] = z[:, :CONV_W] * jax.nn.sigmoid(z[:, CONV_W:])
    rc = min(tt, CONV_ROWS)
    first = CONV_HALO - (CONV_K - 1)
    for r0 in range(0, tt, rc):
        acc = jnp.zeros((rc, CONV_W), F32)
        for j in range(CONV_K):
            acc = acc + cw_ref[j:j + 1, :] * pad_ref[r0 + first + j:r0 + first + j + rc, :]
        y = acc + cb_ref[...]
        mu = jnp.mean(y, axis=-1, keepdims=True)
        d = y - mu
        var = jnp.mean(d * d, axis=-1, keepdims=True)
        y = d * lax.rsqrt(var + LN_EPS) * lg_ref[...] + lb_ref[...]
        y = (y * jax.nn.sigmoid(y)).astype(BF16)
        a_ref[0, r0:r0 + rc, :] = jnp.dot(y, pw_ref[...], preferred_element_type=F32) + pb_ref[...]
    tail = pad_ref[tt:tt + CONV_HALO, :]
    cn_ref[0] = tail
    pad_ref[0:CONV_HALO, :] = tail


def _conv(zc, state, cw, cb, lg, lb, pw, pb, *, tt):
    b, t, _ = zc.shape
    return pl.pallas_call(
        functools.partial(_conv_kernel, tt=tt),
        grid=(b, t // tt),
        in_specs=[pl.BlockSpec((1, tt, 2 * CONV_W), lambda i, j: (i, j, 0)),
                  pl.BlockSpec((1, CONV_HALO, CONV_W), lambda i, j: (i, 0, 0)),
                  _const_spec((CONV_K, CONV_W)), _const_spec((1, CONV_W)), _const_spec((1, CONV_W)),
                  _const_spec((1, CONV_W)), _const_spec((CONV_W, CONV_W)), _const_spec((1, CONV_W))],
        out_specs=[pl.BlockSpec((1, tt, CONV_W), lambda i, j: (i, j, 0)),
                   pl.BlockSpec((1, CONV_HALO, CONV_W), lambda i, j: (i, 0, 0))],
        out_shape=[jax.ShapeDtypeStruct((b, t, CONV_W), F32),
                   jax.ShapeDtypeStruct((b, CONV_HALO, CONV_W), F32)],
        scratch_shapes=[pltpu.VMEM((CONV_HALO + tt, CONV_W), F32)],
        compiler_params=_params(("parallel", "arbitrary")),
        name="conv_module",
    )(zc, state, cw, cb, lg, lb, pw, pb)


def _lam(lam_ref, lam_init):
    lp = lam_ref[...]
    s1 = jnp.sum(lp[0:1] * lp[1:2], axis=-1, keepdims=True)
    s2 = jnp.sum(lp[2:3] * lp[3:4], axis=-1, keepdims=True)
    return jnp.exp(s1) - jnp.exp(s2) + lam_init


def _diff_combine(acc0, l0, acc1, l1, lam, g, lam_init):
    o = acc0 / l0 - lam * (acc1 / l1)
    return _rms(o, g) * (1.0 - lam_init)


def _attn_prompt_kernel(q_ref, k_ref, vt_ref, lam_ref, g_ref, o_ref, acc_ref, sa_ref, sb_ref, *, lam_init):
    qi = pl.program_id(1)
    tq = ATT_QBLOCK
    qt = (q_ref[...] * LOG2E).T
    first = lax.broadcasted_iota(jnp.int32, qt.shape, 0) < ATT_DH
    qc = (jnp.where(first, qt, 0.0).astype(BF16), jnp.where(first, 0.0, qt).astype(BF16))
    acc_ref[...] = jnp.zeros(acc_ref.shape, F32)

    def logits(kc, s_ref):
        kj = k_ref[kc]
        for c in range(2):
            s_ref[c] = jnp.dot(kj, qc[c], preferred_element_type=F32)

    def update(kc, s_ref, carry, masked):
        vj = vt_ref[kc]
        out = []
        for c in range(2):
            m_old, l_old = carry[c]
            s = s_ref[c]
            if masked:
                key = lax.broadcasted_iota(jnp.int32, s.shape, 0) + (kc * ATT_BLOCK - qi * tq)
                qry = lax.broadcasted_iota(jnp.int32, s.shape, 1)
                s = jnp.where(key <= qry, s, NEG_INF)
            m_new = jnp.maximum(m_old, jnp.max(s, axis=0, keepdims=True))
            alpha = jnp.exp2(m_old - m_new)
            p = jnp.exp2(s - m_new)
            l_new = alpha * l_old + jnp.sum(p, axis=0, keepdims=True)
            acc_ref[c] = alpha * acc_ref[c] + jnp.dot(vj, p.astype(BF16), preferred_element_type=F32)
            out.append((m_new, l_new))
        return tuple(out)

    last = qi

    def pair(j, carry):
        logits(2 * j + 1, sb_ref)
        carry = update(2 * j, sa_ref, carry, False)
        logits(2 * j + 2, sa_ref)
        return update(2 * j + 1, sb_ref, carry, False)

    logits(0, sa_ref)
    init = ((jnp.full((1, tq), NEG_INF, F32), jnp.zeros((1, tq), F32)),) * 2
    carry = lax.fori_loop(0, last, pair, init)
    logits(2 * last + 1, sb_ref)
    carry = update(2 * last, sa_ref, carry, True)
    (_, l0), (_, l1) = update(2 * last + 1, sb_ref, carry, True)
    o_t = acc_ref[0] / l0 - _lam(lam_ref, lam_init) * (acc_ref[1] / l1)
    o_ref[...] = _rms(o_t.T, g_ref[...]) * (1.0 - lam_init)


def _attn_prompt(q, kb, vtb, lam_p, g, *, lam_init):
    n = q.shape[0]
    nc = n // ATT_BLOCK
    return pl.pallas_call(
        functools.partial(_attn_prompt_kernel, lam_init=lam_init),
        grid=(ATT_HEADS, n // ATT_QBLOCK),
        in_specs=[pl.BlockSpec((ATT_QBLOCK, HEAD_W), lambda h, i: (i, h)),
                  pl.BlockSpec((nc, ATT_BLOCK, HEAD_W), lambda h, i: (0, 0, h)),
                  pl.BlockSpec((nc, HEAD_W, ATT_BLOCK), lambda h, i: (0, h, 0)),
                  pl.BlockSpec((4, ATT_DH), lambda h, i: (0, 0)),
                  pl.BlockSpec((1, HEAD_W), lambda h, i: (0, 0))],
        out_specs=pl.BlockSpec((ATT_QBLOCK, HEAD_W), lambda h, i: (i, h)),
        out_shape=jax.ShapeDtypeStruct((n, ATT_W), F32),
        scratch_shapes=[pltpu.VMEM((2, HEAD_W, ATT_QBLOCK), F32)]
                       + [pltpu.VMEM((2, ATT_BLOCK, ATT_QBLOCK), F32)] * 2,
        compiler_params=_params(("parallel", "arbitrary")),
        name="attn_prompt",
    )(q, kb, vtb, lam_p, g)


def _attn_sample_kernel(pt_ref, q_ref, kn_ref, vn_ref, *rest, lam_init, t_new):
    del pt_ref
    k_refs = rest[:PAGES_PER_STEP]
    v_refs = rest[PAGES_PER_STEP:2 * PAGES_PER_STEP]
    lam_ref, g_ref, o_ref, m_ref, l_ref, acc_ref = rest[2 * PAGES_PER_STEP:]
    step = pl.program_id(1)
    rows = ATT_HEADS * 2 * t_new
    page_cols = PAGE_SIZE * ATT_HEADS

    q = q_ref[...]
    first = lax.broadcasted_iota(jnp.int32, (t_new, HEAD_W), 1) < ATT_DH
    pieces = []
    for h in range(ATT_HEADS):
        qh = q[:, h * HEAD_W:(h + 1) * HEAD_W]
        pieces += [jnp.where(first, qh, 0.0), jnp.where(first, 0.0, qh)]
    qm = jnp.concatenate(pieces, axis=0).astype(BF16)

    def update(s, v):
        m_old = m_ref[...]
        m_new = jnp.maximum(m_old, jnp.max(s, axis=-1, keepdims=True))
        alpha = jnp.exp(m_old - m_new)
        p = jnp.exp(s - m_new)
        l_ref[...] = alpha * l_ref[...] + jnp.sum(p, axis=-1, keepdims=True)
        acc_ref[...] = alpha * acc_ref[...] + jnp.dot(p.astype(BF16), v, preferred_element_type=F32)
        m_ref[...] = m_new

    def logits(keys):
        return lax.dot_general(qm, keys, (((1,), (1,)), ((), ())), preferred_element_type=F32)

    @pl.when(step == 0)
    def _():
        m_ref[...] = jnp.full(m_ref.shape, NEG_INF, F32)
        l_ref[...] = jnp.zeros(l_ref.shape, F32)
        acc_ref[...] = jnp.zeros(acc_ref.shape, F32)
        zpad = jnp.zeros((PAGE_SIZE - ATT_HEADS * t_new, HEAD_W), F32)
        stack = lambda ref: jnp.concatenate(
            [ref[:, h * HEAD_W:(h + 1) * HEAD_W] for h in range(ATT_HEADS)] + [zpad], axis=0).astype(BF16)
        col = lax.broadcasted_iota(jnp.int32, (rows, PAGE_SIZE), 1)
        row = lax.broadcasted_iota(jnp.int32, (rows, PAGE_SIZE), 0)
        valid = (col // t_new == row // (2 * t_new)) & (col % t_new <= row % t_new)
        update(jnp.where(valid, logits(stack(kn_ref)), NEG_INF), stack(vn_ref))

    col = lax.broadcasted_iota(jnp.int32, (rows, page_cols), 1)
    row = lax.broadcasted_iota(jnp.int32, (rows, page_cols), 0)
    bias = jnp.where(col % ATT_HEADS == row // (2 * t_new), 0.0, NEG_INF)
    bias = jnp.concatenate([bias] * PAGES_PER_STEP, axis=1)
    keys = jnp.concatenate([r[...] for r in k_refs], axis=0).astype(BF16)
    vals = jnp.concatenate([r[...] for r in v_refs], axis=0).astype(BF16)
    update(logits(keys) + bias, vals)

    @pl.when(step == pl.num_programs(1) - 1)
    def _():
        lam = _lam(lam_ref, lam_init)
        acc = acc_ref[...]
        l = l_ref[...]
        outs = []
        for h in range(ATT_HEADS):
            r0, r1, r2 = 2 * h * t_new, (2 * h + 1) * t_new, (2 * h + 2) * t_new
            outs.append(_diff_combine(acc[r0:r1], l[r0:r1], acc[r1:r2], l[r1:r2], lam, g_ref[...], lam_init))
        o_ref[...] = jnp.concatenate(outs, axis=-1)


def _attn_sample(q, k_new, v_new, cache_k, cache_v, page_table, lam_p, g, *, layer, lam_init, t_new):
    n = q.shape[0]
    b = n // t_new
    n_pages = page_table.shape[1]
    steps = n_pages // PAGES_PER_STEP
    pool = cache_k.shape[1]
    ck = cache_k.reshape(DEPTH, pool, PAGE_SIZE * ATT_HEADS, HEAD_W)
    cv = cache_v.reshape(DEPTH, pool, PAGE_SIZE * ATT_HEADS, HEAD_W)
    row = pl.BlockSpec((t_new, ATT_W), lambda i, s, pt: (i, 0))

    def page_spec(j):
        return pl.BlockSpec((None, None, PAGE_SIZE * ATT_HEADS, HEAD_W),
                            lambda i, s, pt: (layer, pt[i, s * PAGES_PER_STEP + j], 0, 0))

    pages = [page_spec(j) for j in range(PAGES_PER_STEP)]
    rows = ATT_HEADS * 2 * t_new
    grid_spec = pltpu.PrefetchScalarGridSpec(
        num_scalar_prefetch=1,
        grid=(b, steps),
        in_specs=[row, row, row] + pages + pages + [
            pl.BlockSpec((4, ATT_DH), lambda i, s, pt: (0, 0)),
            pl.BlockSpec((1, HEAD_W), lambda i, s, pt: (0, 0))],
        out_specs=row,
        scratch_shapes=[pltpu.VMEM((rows, 1), F32), pltpu.VMEM((rows, 1), F32),
                        pltpu.VMEM((rows, HEAD_W), F32)],
    )
    return pl.pallas_call(
        functools.partial(_attn_sample_kernel, lam_init=lam_init, t_new=t_new),
        grid_spec=grid_spec,
        out_shape=jax.ShapeDtypeStruct((n, ATT_W), F32),
        compiler_params=_params(("parallel", "arbitrary")),
        name="attn_sample",
    )(page_table, q, k_new, v_new, *([ck] * PAGES_PER_STEP), *([cv] * PAGES_PER_STEP), lam_p, g)


def _rwkv_kernel(z_ref, sh_ref, s0_ref, mu_ref, w0_ref, w2_ref, a0_ref, a2_ref, g2_ref,
                 kk_ref, ka_ref, rk_ref, lng_ref, lnb_ref, seg_ref,
                 c_ref, shn_ref, sn_ref,
                 carry_ref, st_ref, r_s, w_s, k_s, kk_s, b_s, vt_s, yt_s, *, n_valid):
    tt = SCAN_TILE

    @pl.when(pl.program_id(1) == 0)
    def _():
        carry_ref[...] = sh_ref[0]
        st_ref[...] = s0_ref[0]

    z = z_ref[0]
    rowi = lax.broadcasted_iota(jnp.int32, z.shape, 0)
    prev = jnp.where(rowi == 0, carry_ref[...], pltpu.roll(z, 1, 0))
    shn_ref[0] = z[n_valid - 1:n_valid, :]
    carry_ref[...] = z[tt - 1:tt, :]
    zs = z + (prev - z) * mu_ref[...]

    def segsum(x):
        return jnp.dot(x, seg_ref[...], precision=HI, preferred_element_type=F32)

    r = zs[:, 0:RWKV_W]
    k = zs[:, RWKV_W:2 * RWKV_W]
    v = zs[:, 2 * RWKV_W:3 * RWKV_W]
    wa = zs[:, 3 * RWKV_W:3 * RWKV_W + W_LORA + A_LORA]
    gd = zs[:, 3 * RWKV_W + W_LORA + A_LORA:]
    wx = w0_ref[...] + jnp.dot(jnp.tanh(wa), w2_ref[...], precision=HI, preferred_element_type=F32)
    softplus = jnp.maximum(-wx, 0.0) + jnp.log1p(jnp.exp(-jnp.abs(wx)))
    w_log = -softplus - 0.5
    decay = jnp.exp(-jnp.exp(w_log))
    a = jax.nn.sigmoid(a0_ref[...] + jnp.dot(wa, a2_ref[...], precision=HI, preferred_element_type=F32))
    g = jnp.dot(jax.nn.sigmoid(gd), g2_ref[...], precision=HI, preferred_element_type=F32)
    kk = k * kk_ref[...]
    kk = kk * lax.rsqrt(segsum(kk * kk) + 1e-12)
    k = k * (1.0 + (a - 1.0) * ka_ref[...])
    bonus = segsum(r * k * rk_ref[...]) * v

    r_s[...] = r
    w_s[...] = decay
    k_s[...] = k
    kk_s[...] = kk
    b_s[...] = kk * a
    vt_s[...] = v.T
    yt_s[...] = jnp.zeros(yt_s.shape, F32)

    lane = lax.broadcasted_iota(jnp.int32, (RWKV_DH, 2 * RWKV_DH), 1)
    lo = lane < RWKV_DH

    def pair_sum(x):
        s_lo = jnp.sum(jnp.where(lo, x, 0.0), axis=-1, keepdims=True)
        s_hi = jnp.sum(jnp.where(lo, 0.0, x), axis=-1, keepdims=True)
        return s_lo, s_hi

    def step(t, rows8, i, states):
        hot = lane == t
        new_states = []
        for p in range(2):
            sl = slice(p * 2 * RWKV_DH, (p + 1) * 2 * RWKV_DH)
            row_t = lambda name: rows8[name][i:i + 1, sl]
            s_old = states[p]
            sa_lo, sa_hi = pair_sum(s_old * row_t("kk"))
            h0 = slice(2 * p * RWKV_DH, (2 * p + 1) * RWKV_DH)
            h1 = slice((2 * p + 1) * RWKV_DH, (2 * p + 2) * RWKV_DH)
            v_lo = jnp.sum(jnp.where(hot, vt_s[h0, :], 0.0), axis=-1, keepdims=True)
            v_hi = jnp.sum(jnp.where(hot, vt_s[h1, :], 0.0), axis=-1, keepdims=True)
            s_new = (s_old * row_t("w") - jnp.where(lo, sa_lo, sa_hi) * row_t("b")
                     + jnp.where(lo, v_lo, v_hi) * row_t("k"))
            y_lo, y_hi = pair_sum(s_new * row_t("r"))
            yt_s[h0, :] = jnp.where(hot, y_lo, yt_s[h0, :])
            yt_s[h1, :] = jnp.where(hot, y_hi, yt_s[h1, :])
            new_states.append(s_new)
        return tuple(new_states)

    sub = 8

    def group(gi, states):
        base = pl.multiple_of(gi * sub, sub)
        rows8 = {name: ref[pl.ds(base, sub), :]
                 for name, ref in (("r", r_s), ("w", w_s), ("k", k_s), ("kk", kk_s), ("b", b_s))}
        for i in range(sub):
            states = step(base + i, rows8, i, states)
        return states

    final = lax.fori_loop(0, n_valid // sub, group, (st_ref[0], st_ref[1]))
    st_ref[0] = final[0]
    st_ref[1] = final[1]
    sn_ref[0] = st_ref[...]

    y = yt_s[...].T
    mean = segsum(y) * (1.0 / RWKV_DH)
    d = y - mean
    var = segsum(d * d) * (1.0 / RWKV_DH)
    y = d * lax.rsqrt(var + RWKV_GN_EPS) * lng_ref[...] + lnb_ref[...]
    c_ref[0] = (y + bonus) * g


def _rwkv(z, shift, state, mu, w0, w2p, a0, a2p, g2, kk, ka, rk, lng, lnb, seg, *, n_valid):
    b, t, _ = z.shape
    tt = SCAN_TILE
    vec = lambda w: _const_spec((1, w))
    return pl.pallas_call(
        functools.partial(_rwkv_kernel, n_valid=n_valid),
        grid=(b, t // tt),
        in_specs=[pl.BlockSpec((1, tt, RWKV_IN), lambda i, j: (i, j, 0)),
                  pl.BlockSpec((1, 1, RWKV_IN), lambda i, j: (i, 0, 0)),
                  pl.BlockSpec((1, 2, RWKV_DH, 2 * RWKV_DH), lambda i, j: (i, 0, 0, 0)),
                  vec(RWKV_IN), vec(RWKV_W), _const_spec((W_LORA + A_LORA, RWKV_W)), vec(RWKV_W),
                  _const_spec((W_LORA + A_LORA, RWKV_W)), _const_spec((G_LORA, RWKV_W)),
                  vec(RWKV_W), vec(RWKV_W), vec(RWKV_W), vec(RWKV_W), vec(RWKV_W),
                  _const_spec((RWKV_W, RWKV_W))],
        out_specs=[pl.BlockSpec((1, tt, RWKV_W), lambda i, j: (i, j, 0)),
                   pl.BlockSpec((1, 1, RWKV_IN), lambda i, j: (i, 0, 0)),
                   pl.BlockSpec((1, 2, RWKV_DH, 2 * RWKV_DH), lambda i, j: (i, 0, 0, 0))],
        out_shape=[jax.ShapeDtypeStruct((b, t, RWKV_W), F32),
                   jax.ShapeDtypeStruct((b, 1, RWKV_IN), F32),
                   jax.ShapeDtypeStruct((b, 2, RWKV_DH, 2 * RWKV_DH), F32)],
        scratch_shapes=[pltpu.VMEM((1, RWKV_IN), F32), pltpu.VMEM((2, RWKV_DH, 2 * RWKV_DH), F32)]
                       + [pltpu.VMEM((tt, RWKV_W), F32)] * 5
                       + [pltpu.VMEM((RWKV_W, tt), F32)] * 2,
        compiler_params=_params(("parallel", "arbitrary")),
        name="rwkv7",
    )(z, shift, state, mu, w0, w2p, a0, a2p, g2, kk, ka, rk, lng, lnb, seg)


RW_CHUNK = 64
RW_TILE = 256
SUBLANES = 8
NN = (((1,), (0,)), ((), ()))
NT = (((1,), (1,)), ((), ()))


def _split(a):
    hi = a.astype(BF16)
    return hi, (a - hi.astype(F32)).astype(BF16)


def _dot3s(a, b, dims=NN):
    d = lambda x, y: lax.dot_general(x, y, dims, preferred_element_type=F32)
    return d(a[0], b[0]) + (d(a[0], b[1]) + d(a[1], b[0]))


def _dot3(a, b, dims=NN):
    return _dot3s(_split(a), _split(b), dims)


def _rwkv_chunk_kernel(z_ref, sh_ref, s0_ref, mu_ref, w0_ref, w2_ref, a0_ref, a2_ref, g2_ref,
                       kk_ref, ka_ref, rk_ref, lng_ref, lnb_ref, seg_ref,
                       c_ref, shn_ref, sn_ref, carry_ref, st_ref, y_s):
    tt, cs = RW_TILE, RW_CHUNK

    @pl.when(pl.program_id(1) == 0)
    def _():
        carry_ref[...] = sh_ref[0]
        st_ref[...] = s0_ref[0]

    z = z_ref[0]
    rowi = lax.broadcasted_iota(jnp.int32, z.shape, 0)
    prev = jnp.where(rowi == 0, carry_ref[...], pltpu.roll(z, 1, 0))
    shn_ref[0] = z[tt - 1:tt, :]
    carry_ref[...] = z[tt - 1:tt, :]
    zs = z + (prev - z) * mu_ref[...]

    seg = _split(seg_ref[...])

    def segsum(x):
        return _dot3s(_split(x), seg)

    r = zs[:, 0:RWKV_W]
    k = zs[:, RWKV_W:2 * RWKV_W]
    v = zs[:, 2 * RWKV_W:3 * RWKV_W]
    wa = zs[:, 3 * RWKV_W:3 * RWKV_W + W_LORA + A_LORA]
    gd = zs[:, 3 * RWKV_W + W_LORA + A_LORA:]
    wx = w0_ref[...] + _dot3(jnp.tanh(wa), w2_ref[...])
    softplus = jnp.maximum(-wx, 0.0) + jnp.log1p(jnp.exp(-jnp.abs(wx)))
    logw = -jnp.exp(-softplus - 0.5)
    a = jax.nn.sigmoid(a0_ref[...] + _dot3(wa, a2_ref[...]))
    g = _dot3(jax.nn.sigmoid(gd), g2_ref[...])
    kk = k * kk_ref[...]
    kk = kk * lax.rsqrt(segsum(kk * kk) + 1e-12)
    k = k * (1.0 + (a - 1.0) * ka_ref[...])
    bonus = segsum(r * k * rk_ref[...]) * v
    b = kk * a

    lane_head = lax.broadcasted_iota(jnp.int32, (1, RWKV_W), 1) // RWKV_DH
    head_mask = [lane_head == h for h in range(RWKV_HEADS)]
    row = lax.broadcasted_iota(jnp.int32, (2 * cs, cs), 0)
    col = lax.broadcasted_iota(jnp.int32, (2 * cs, cs), 1)
    tril2 = col <= jnp.where(row < cs, row - 1, row - cs)
    tri_incl = _split((lax.broadcasted_iota(jnp.int32, (cs, cs), 0)
                       >= lax.broadcasted_iota(jnp.int32, (cs, cs), 1)).astype(F32))
    bd_row = lax.broadcasted_iota(jnp.int32, (RWKV_W, RWKV_W), 0) // RWKV_DH
    bd_col = lax.broadcasted_iota(jnp.int32, (RWKV_W, RWKV_W), 1) // RWKV_DH
    block_diag = bd_row == bd_col
    lo_half = lax.broadcasted_iota(jnp.int32, (SUBLANES, 2 * RWKV_DH), 1) < RWKV_DH
    n_sub = cs // SUBLANES

    for ci in range(tt // cs):
        sl = slice(ci * cs, (ci + 1) * cs)
        r_c, lw_c, k_c, v_c, kk_c, b_c = r[sl], logw[sl], k[sl], v[sl], kk[sl], b[sl]
        cum = _dot3s(tri_incl, _split(lw_c))
        total = cum[cs - 1:cs, :]
        inv = jnp.exp(-cum)
        fin = jnp.exp(total - cum)
        lhs = _split(jnp.concatenate([kk_c * jnp.exp(cum - lw_c), r_c * jnp.exp(cum)], axis=0))
        kh, bh = _split(k_c * inv), _split(b_c * inv)
        v_s = _split(v_c)

        wy = jnp.zeros((2 * cs, RWKV_W), F32)
        l_mats, yb_mats = [], []
        for h in range(RWKV_HEADS):
            lhs_h = (jnp.where(head_mask[h], lhs[0], 0), jnp.where(head_mask[h], lhs[1], 0))
            gk = jnp.where(tril2, _dot3s(lhs_h, kh, NT), 0.0)
            gb = jnp.where(tril2, _dot3s(lhs_h, bh, NT), 0.0)
            wy = wy + jnp.where(head_mask[h], _dot3s(_split(gk), v_s), 0.0)
            l_mats.append(gb[:cs])
            yb_mats.append(gb[cs:])

        state = st_ref[...]
        sm = _dot3s(lhs, _split(state), NT)
        rhs = sm[:cs] + wy[:cs]
        tiles = [rhs[i * SUBLANES:(i + 1) * SUBLANES] for i in range(n_sub)]
        for j in range(cs):
            i0 = j // SUBLANES
            u_j = tiles[i0][j % SUBLANES:j % SUBLANES + 1, :]
            for i in range(i0, n_sub):
                rows = slice(i * SUBLANES, (i + 1) * SUBLANES)
                bc = lambda h: jnp.broadcast_to(l_mats[h][rows, j:j + 1], (SUBLANES, 2 * RWKV_DH))
                l_col = jnp.concatenate([jnp.where(lo_half, bc(0), bc(1)),
                                         jnp.where(lo_half, bc(2), bc(3))], axis=1)
                tiles[i] = tiles[i] - l_col * u_j
        u = jnp.concatenate(tiles, axis=0)
        u_s = _split(u)
        bu = jnp.zeros((cs, RWKV_W), F32)
        for h in range(RWKV_HEADS):
            bu = bu + jnp.where(head_mask[h], _dot3s(_split(yb_mats[h]), u_s), 0.0)
        y_s[sl, :] = sm[cs:] + wy[cs:] - bu
        delta = _dot3s(_split(v_c.T), _split(k_c * fin)) - _dot3s(_split(u.T), _split(b_c * fin))
        st_ref[...] = state * jnp.exp(total) + jnp.where(block_diag, delta, 0.0)

    sn_ref[0] = st_ref[...]
    y = y_s[...]
    mean = segsum(y) * (1.0 / RWKV_DH)
    d = y - mean
    var = segsum(d * d) * (1.0 / RWKV_DH)
    y = d * lax.rsqrt(var + RWKV_GN_EPS) * lng_ref[...] + lnb_ref[...]
    c_ref[0] = (y + bonus) * g


def _rwkv_chunked(z, shift, state, mu, w0, w2p, a0, a2p, g2, kk, ka, rk, lng, lnb, seg):
    b, t, _ = z.shape
    tt = RW_TILE
    vec = lambda w: _const_spec((1, w))
    return pl.pallas_call(
        _rwkv_chunk_kernel,
        grid=(b, t // tt),
        in_specs=[pl.BlockSpec((1, tt, RWKV_IN), lambda i, j: (i, j, 0)),
                  pl.BlockSpec((1, 1, RWKV_IN), lambda i, j: (i, 0, 0)),
                  pl.BlockSpec((1, RWKV_W, RWKV_W), lambda i, j: (i, 0, 0)),
                  vec(RWKV_IN), vec(RWKV_W), _const_spec((W_LORA + A_LORA, RWKV_W)), vec(RWKV_W),
                  _const_spec((W_LORA + A_LORA, RWKV_W)), _const_spec((G_LORA, RWKV_W)),
                  vec(RWKV_W), vec(RWKV_W), vec(RWKV_W), vec(RWKV_W), vec(RWKV_W),
                  _const_spec((RWKV_W, RWKV_W))],
        out_specs=[pl.BlockSpec((1, tt, RWKV_W), lambda i, j: (i, j, 0)),
                   pl.BlockSpec((1, 1, RWKV_IN), lambda i, j: (i, 0, 0)),
                   pl.BlockSpec((1, RWKV_W, RWKV_W), lambda i, j: (i, 0, 0))],
        out_shape=[jax.ShapeDtypeStruct((b, t, RWKV_W), F32),
                   jax.ShapeDtypeStruct((b, 1, RWKV_IN), F32),
                   jax.ShapeDtypeStruct((b, RWKV_W, RWKV_W), F32)],
        scratch_shapes=[pltpu.VMEM((1, RWKV_IN), F32), pltpu.VMEM((RWKV_W, RWKV_W), F32),
                        pltpu.VMEM((tt, RWKV_W), F32)],
        compiler_params=_params(("parallel", "arbitrary")),
        name="rwkv7_chunked",
    )(z, shift, state, mu, w0, w2p, a0, a2p, g2, kk, ka, rk, lng, lnb, seg)


def _block_diag_state(s):
    b = s.shape[0]
    eye = jnp.eye(RWKV_HEADS, dtype=s.dtype)
    return jnp.einsum("bhvk,hg->bhvgk", s, eye).reshape(b, RWKV_W, RWKV_W)


def _diag_blocks(s):
    b = s.shape[0]
    s = s.reshape(b, RWKV_HEADS, RWKV_DH, RWKV_HEADS, RWKV_DH)
    return jnp.stack([s[:, h, :, h, :] for h in range(RWKV_HEADS)], axis=1)


def _pair_state(s):
    b = s.shape[0]
    s = s.reshape(b, 2, 2, RWKV_DH, RWKV_DH)
    return jnp.transpose(s, (0, 1, 3, 2, 4)).reshape(b, 2, RWKV_DH, 2 * RWKV_DH)


def _unpair_state(s):
    b = s.shape[0]
    s = s.reshape(b, 2, RWKV_DH, 2, RWKV_DH)
    return jnp.transpose(s, (0, 1, 3, 2, 4)).reshape(b, RWKV_HEADS, RWKV_DH, RWKV_DH)


def _out_ffn_kernel(x_ref, a_ref, b_ref, c_ref, gs_ref, wo_ref, g2_ref, wg_ref, wu_ref, wd_ref,
                    fg_ref, o_ref, acc_ref, *, final):
    mix = jnp.concatenate([a_ref[...], b_ref[...], c_ref[...]], axis=-1) * gs_ref[...]
    x1 = x_ref[...] + jnp.dot(mix.astype(BF16), wo_ref[...], preferred_element_type=F32)
    acc_ref[...] = x1
    h2 = _rms(x1, g2_ref[...]).astype(BF16)
    for c in range(0, D_FF, FF_CHUNK):
        gate = jnp.dot(h2, wg_ref[:, c:c + FF_CHUNK], preferred_element_type=F32)
        up = jnp.dot(h2, wu_ref[:, c:c + FF_CHUNK], preferred_element_type=F32)
        act = (gate * jax.nn.sigmoid(gate) * up).astype(BF16)
        acc_ref[...] += jnp.dot(act, wd_ref[c:c + FF_CHUNK, :], preferred_element_type=F32)
    x2 = acc_ref[...]
    o_ref[...] = _rms(x2, fg_ref[...]) if final else x2


def _out_ffn(x, a, b, c, gs, wo, g2, wg, wu, wd, fg, *, tm, final):
    n = x.shape[0]
    row = lambda w: pl.BlockSpec((tm, w), lambda i: (i, 0))
    return pl.pallas_call(
        functools.partial(_out_ffn_kernel, final=final),
        grid=(n // tm,),
        in_specs=[row(D_MODEL), row(CONV_W), row(ATT_W), row(RWKV_W), _const_spec((1, D_MODEL)),
                  _const_spec((D_MODEL, D_MODEL)), _const_spec((1, D_MODEL)),
                  _const_spec((D_MODEL, D_FF)), _const_spec((D_MODEL, D_FF)),
                  _const_spec((D_FF, D_MODEL)), _const_spec((1, D_MODEL))],
        out_specs=row(D_MODEL),
        out_shape=jax.ShapeDtypeStruct((n, D_MODEL), F32),
        scratch_shapes=[pltpu.VMEM((tm, D_MODEL), F32)],
        compiler_params=_params(("parallel",)),
        name="out_ffn",
    )(x, a, b, c, gs, wo, g2, wg, wu, wd, fg)


def _rope_tables(pos):
    half = ATT_DH // 2
    inv = ROPE_THETA ** (-jnp.arange(half, dtype=F32) / half)
    ang = pos.astype(F32)[:, None] * inv[None, :]
    tile = lambda t: jnp.concatenate([t] * (HEAD_W // half), axis=-1)
    return tile(jnp.cos(ang)), tile(jnp.sin(ang))


def _rotate_half_columns(w):
    half = ATT_DH // 2
    w = w.reshape(w.shape[0], ATT_W // ATT_DH, 2, half)
    return jnp.concatenate([-w[:, :, 1], w[:, :, 0]], axis=-1).reshape(w.shape[0], ATT_W)


def kernel(x_prompt, x_sample, cache_k, cache_v, state_conv, state_shift, state_wkv, page_table, norm1_g, w_in, conv_w, conv_b, conv_ln_g, conv_ln_b, conv_pw_w, conv_pw_b, att_lambda, att_subln_g, rwkv_mu, rwkv_w0, rwkv_w2, rwkv_a0, rwkv_a2, rwkv_g2, rwkv_kk, rwkv_ka, rwkv_rk, rwkv_ln_g, rwkv_ln_b, group_scale, w_out, norm2_g, w_gate, w_up, w_down, final_g):
    bp, tp, _ = x_prompt.shape
    bs, ts, _ = x_sample.shape
    assert bp == 1 and tp % ATT_QBLOCK == 0 and ts <= SCAN_TILE and ts % 8 == 0
    past_len = page_table.shape[1] * PAGE_SIZE
    groups = {
        "p": dict(b=bp, t=tp, tm=512, conv_tt=256, x=x_prompt.reshape(bp * tp, D_MODEL),
                  pos=jnp.arange(tp, dtype=jnp.int32)),
        "s": dict(b=bs, t=ts, tm=bs * ts, conv_tt=ts, x=x_sample.reshape(bs * ts, D_MODEL),
                  pos=jnp.tile(past_len + jnp.arange(ts, dtype=jnp.int32), bs)),
    }
    for grp in groups.values():
        grp["cos"], grp["sin"] = _rope_tables(grp["pos"])
    outs = {name: dict(k=[], v=[], conv=[], shift=[], wkv=[]) for name in groups}

    seg = jnp.asarray(np.kron(np.eye(RWKV_HEADS, dtype=np.float32),
                              np.ones((RWKV_DH, RWKV_DH), np.float32)))
    row = lambda p: p.reshape(1, -1)

    for l in range(DEPTH):
        lam_init = 0.8 - 0.6 * math.exp(-0.3 * l)
        att_w = w_in[l][:, 2 * CONV_W:2 * CONV_W + 2 * ATT_W]
        w_ext = jnp.concatenate([w_in[l], _rotate_half_columns(att_w[:, :ATT_W]),
                                 _rotate_half_columns(att_w[:, ATT_W:])], axis=-1).astype(BF16)
        w2p = jnp.concatenate([rwkv_w2[l], jnp.zeros((A_LORA, RWKV_W), F32)], axis=0)
        a2p = jnp.concatenate([jnp.zeros((W_LORA, RWKV_W), F32), rwkv_a2[l]], axis=0)
        subln = row(att_subln_g[l])
        for name, grp in groups.items():
            b, t = grp["b"], grp["t"]
            prompt = name == "p"
            zc, q, k, v, zr, *kv_bf = _in_proj(grp["x"], row(norm1_g[l]), w_ext, grp["cos"], grp["sin"],
                                               tm=grp["tm"], emit_bf16=prompt)
            if prompt:
                conv_state = jnp.zeros((b, CONV_HALO, CONV_W), F32)
            else:
                conv_state = jnp.pad(state_conv[l], ((0, 0), (CONV_HALO - (CONV_K - 1), 0), (0, 0)))
            a_out, conv_new = _conv(zc.reshape(b, t, 2 * CONV_W), conv_state, conv_w[l].reshape(CONV_K, CONV_W),
                                    row(conv_b[l]), row(conv_ln_g[l]), row(conv_ln_b[l]),
                                    conv_pw_w[l].astype(BF16), row(conv_pw_b[l]), tt=grp["conv_tt"])
            if prompt:
                b_out = _attn_prompt(q, kv_bf[0], kv_bf[1], att_lambda[l], subln, lam_init=lam_init)
            else:
                b_out = _attn_sample(q, k, v, cache_k, cache_v, page_table, att_lambda[l], subln,
                                     layer=l, lam_init=lam_init, t_new=t)
            zr3 = zr.reshape(b, t, RWKV_IN)
            rw_params = (row(rwkv_mu[l]), row(rwkv_w0[l]), w2p, row(rwkv_a0[l]), a2p, rwkv_g2[l],
                         row(rwkv_kk[l]), row(rwkv_ka[l]), row(rwkv_rk[l]), row(rwkv_ln_g[l]),
                         row(rwkv_ln_b[l]), seg)
            if prompt:
                c_out, shift_new, wkv_new = _rwkv_chunked(
                    zr3, jnp.zeros((b, 1, RWKV_IN), F32), jnp.zeros((b, RWKV_W, RWKV_W), F32), *rw_params)
                wkv_new = _diag_blocks(wkv_new)
            else:
                zr3 = jnp.pad(zr3, ((0, 0), (0, SCAN_TILE - t), (0, 0)))
                c_out, shift_new, wkv_new = _rwkv(
                    zr3, state_shift[l].reshape(b, 1, RWKV_IN), _pair_state(state_wkv[l]), *rw_params,
                    n_valid=t)
                wkv_new = _unpair_state(wkv_new)
            c_out = c_out[:, :t].reshape(b * t, RWKV_W)
            grp["x"] = _out_ffn(grp["x"], a_out.reshape(b * t, CONV_W), b_out, c_out, row(group_scale[l]),
                                w_out[l].astype(BF16), row(norm2_g[l]), w_gate[l].astype(BF16),
                                w_up[l].astype(BF16), w_down[l].astype(BF16), row(final_g),
                                tm=grp["tm"], final=(l == DEPTH - 1))
            o = outs[name]
            o["k"].append(k.reshape(b, t, ATT_HEADS, HEAD_W))
            o["v"].append(v.reshape(b, t, ATT_HEADS, HEAD_W))
            o["conv"].append(conv_new[:, CONV_HALO - (CONV_K - 1):])
            o["shift"].append(shift_new.reshape(b, RWKV_IN))
            o["wkv"].append(wkv_new)

    y_prompt = groups["p"]["x"].reshape(bp, tp, D_MODEL)
    y_sample = groups["s"]["x"].reshape(bs, ts, D_MODEL)
    st = lambda name, key: jnp.stack(outs[name][key])
    return (y_prompt, y_sample,
            st("p", "k"), st("p", "v"), st("p", "conv"), st("p", "shift"), st("p", "wkv"),
            st("s", "k"), st("s", "v"), st("s", "conv"), st("s", "shift"), st("s", "wkv"))
```

```python
import functools
import math

import jax
import jax.numpy as jnp
import numpy as np
from jax import lax
from jax.experimental import pallas as pl
from jax.experimental.pallas import tpu as pltpu

D_MODEL = 1024
DEPTH = 2
PAGE_SIZE = 128
CONV_W = 256
ATT_W = 512
RWKV_W = 256
ATT_HEADS = 4
ATT_DH = 64
HEAD_W = 2 * ATT_DH
RWKV_DH = 64
RWKV_HEADS = 4
CONV_K = 31
W_LORA = 64
A_LORA = 64
G_LORA = 128
RWKV_IN = 3 * RWKV_W + W_LORA + A_LORA + G_LORA
IN_W = 2 * CONV_W + 3 * ATT_W + RWKV_IN
D_FF = 2816
ROPE_THETA = 10000.0
NORM_EPS = 1e-6
LN_EPS = 1e-5
RWKV_GN_EPS = 64e-5
NEG_INF = -1e30

F32 = jnp.float32
BF16 = jnp.bfloat16
HI = lax.Precision.HIGHEST

CONV_HALO = 32
CONV_ROWS = 64
FF_CHUNK = 256
SCAN_TILE = 128
ATT_BLOCK = 512
ATT_QBLOCK = 2 * ATT_BLOCK
LOG2E = math.log2(math.e)
PAGES_PER_STEP = 16
VMEM_LIMIT = 56 * 1024 * 1024


def _const_spec(shape):
    zeros = (0,) * len(shape)
    return pl.BlockSpec(shape, lambda *_: zeros, pipeline_mode=pl.Buffered(1))


def _params(sem):
    return pltpu.CompilerParams(dimension_semantics=sem, vmem_limit_bytes=VMEM_LIMIT)


def _rms(x, g):
    return x * lax.rsqrt(jnp.mean(x * x, axis=-1, keepdims=True) + NORM_EPS) * g


def _in_proj_kernel(x_ref, g_ref, w_ref, cos_ref, sin_ref, *out_refs, emit_bf16):
    if emit_bf16:
        zc_ref, q_ref, k_ref, v_ref, zr_ref, kb_ref, vb_ref = out_refs
    else:
        zc_ref, q_ref, k_ref, v_ref, zr_ref = out_refs
    h = _rms(x_ref[...], g_ref[...]).astype(BF16)

    def mm(c0, c1):
        return jnp.dot(h, w_ref[:, c0:c1], preferred_element_type=F32)

    cos = jnp.concatenate([cos_ref[...]] * ATT_HEADS, axis=-1)
    sin = jnp.concatenate([sin_ref[...]] * ATT_HEADS, axis=-1)
    q0, k0, v0, r0 = 2 * CONV_W, 2 * CONV_W + ATT_W, 2 * CONV_W + 2 * ATT_W, 2 * CONV_W + 3 * ATT_W
    zc_ref[...] = mm(0, q0)
    q = mm(q0, k0) * cos + mm(IN_W, IN_W + ATT_W) * sin
    q_ref[...] = q * (ATT_DH ** -0.5)
    k = mm(k0, v0) * cos + mm(IN_W + ATT_W, IN_W + 2 * ATT_W) * sin
    v = mm(v0, r0)
    if emit_bf16:
        tm = k.shape[0]
        for head in range(ATT_HEADS):
            lanes = slice(head * HEAD_W, (head + 1) * HEAD_W)
            k_ref[pl.ds(head, tm, stride=ATT_HEADS), :] = k[:, lanes]
            v_ref[pl.ds(head, tm, stride=ATT_HEADS), :] = v[:, lanes]
    else:
        k_ref[...] = k
        v_ref[...] = v
    zr_ref[...] = mm(r0, IN_W)
    if emit_bf16:
        kb_ref[0] = k.astype(BF16)
        vb_ref[0] = v.T.astype(BF16)


def _in_proj(x, g, w_ext, cos, sin, *, tm, emit_bf16):
    n = x.shape[0]
    row = lambda w: pl.BlockSpec((tm, w), lambda i: (i, 0))
    widths = [2 * CONV_W, ATT_W, ATT_W, ATT_W, RWKV_IN]
    out_shape = [jax.ShapeDtypeStruct((n, w), F32) for w in widths]
    out_specs = [row(w) for w in widths]
    if emit_bf16:
        assert tm == ATT_BLOCK
        for i in (2, 3):
            out_shape[i] = jax.ShapeDtypeStruct((n * ATT_HEADS, HEAD_W), F32)
            out_specs[i] = pl.BlockSpec((tm * ATT_HEADS, HEAD_W), lambda i: (i, 0))
        out_shape += [jax.ShapeDtypeStruct((n // tm, tm, ATT_W), BF16),
                      jax.ShapeDtypeStruct((n // tm, ATT_W, tm), BF16)]
        out_specs += [pl.BlockSpec((1, tm, ATT_W), lambda i: (i, 0, 0)),
                      pl.BlockSpec((1, ATT_W, tm), lambda i: (i, 0, 0))]
    return pl.pallas_call(
        functools.partial(_in_proj_kernel, emit_bf16=emit_bf16),
        grid=(n // tm,),
        in_specs=[row(D_MODEL), _const_spec((1, D_MODEL)), _const_spec(w_ext.shape),
                  row(HEAD_W), row(HEAD_W)],
        out_specs=out_specs,
        out_shape=out_shape,
        compiler_params=_params(("parallel",)),
        name="in_proj",
    )(x, g, w_ext, cos, sin)


def _conv_kernel(zc_ref, st_ref, cw_ref, cb_ref, lg_ref, lb_ref, pw_ref, pb_ref,
                 a_ref, cn_ref, pad_ref, *, tt):
    @pl.when(pl.program_id(1) == 0)
    def _():
        pad_ref[0:CONV_HALO, :] = st_ref[0]

    z = zc_ref[0]
    pad_ref[CONV_HALO:CONV_HALO + tt, :] = z[:, :CONV_W] * jax.nn.sigmoid(z[:, CONV_W:])
    rc = min(tt, CONV_ROWS)
    first = CONV_HALO - (CONV_K - 1)
    for r0 in range(0, tt, rc):
        acc = jnp.zeros((rc, CONV_W), F32)
        for j in range(CONV_K):
            acc = acc + cw_ref[j:j + 1, :] * pad_ref[r0 + first + j:r0 + first + j + rc, :]
        y = acc + cb_ref[...]
        mu = jnp.mean(y, axis=-1, keepdims=True)
        d = y - mu
        var = jnp.mean(d * d, axis=-1, keepdims=True)
        y = d * lax.rsqrt(var + LN_EPS) * lg_ref[...] + lb_ref[...]
        y = (y * jax.nn.sigmoid(y)).astype(BF16)
        a_ref[0, r0:r0 + rc, :] = jnp.dot(y, pw_ref[...], preferred_element_type=F32) + pb_ref[...]
    tail = pad_ref[tt:tt + CONV_HALO, :]
    cn_ref[0] = tail
    pad_ref[0:CONV_HALO, :] = tail


def _conv(zc, state, cw, cb, lg, lb, pw, pb, *, tt):
    b, t, _ = zc.shape
    return pl.pallas_call(
        functools.partial(_conv_kernel, tt=tt),
        grid=(b, t // tt),
        in_specs=[pl.BlockSpec((1, tt, 2 * CONV_W), lambda i, j: (i, j, 0)),
                  pl.BlockSpec((1, CONV_HALO, CONV_W), lambda i, j: (i, 0, 0)),
                  _const_spec((CONV_K, CONV_W)), _const_spec((1, CONV_W)), _const_spec((1, CONV_W)),
                  _const_spec((1, CONV_W)), _const_spec((CONV_W, CONV_W)), _const_spec((1, CONV_W))],
        out_specs=[pl.BlockSpec((1, tt, CONV_W), lambda i, j: (i, j, 0)),
                   pl.BlockSpec((1, CONV_HALO, CONV_W), lambda i, j: (i, 0, 0))],
        out_shape=[jax.ShapeDtypeStruct((b, t, CONV_W), F32),
                   jax.ShapeDtypeStruct((b, CONV_HALO, CONV_W), F32)],
        scratch_shapes=[pltpu.VMEM((CONV_HALO + tt, CONV_W), F32)],
        compiler_params=_params(("parallel", "arbitrary")),
        name="conv_module",
    )(zc, state, cw, cb, lg, lb, pw, pb)


def _lam(lam_ref, lam_init):
    lp = lam_ref[...]
    s1 = jnp.sum(lp[0:1] * lp[1:2], axis=-1, keepdims=True)
    s2 = jnp.sum(lp[2:3] * lp[3:4], axis=-1, keepdims=True)
    return jnp.exp(s1) - jnp.exp(s2) + lam_init


def _diff_combine(acc0, l0, acc1, l1, lam, g, lam_init):
    o = acc0 / l0 - lam * (acc1 / l1)
    return _rms(o, g) * (1.0 - lam_init)


def _attn_prompt_kernel(q_ref, k_ref, vt_ref, lam_ref, g_ref, o_ref, acc_ref, sa_ref, sb_ref, *, lam_init):
    qi = pl.program_id(1)
    tq = ATT_QBLOCK
    qt = (q_ref[...] * LOG2E).T
    first = lax.broadcasted_iota(jnp.int32, qt.shape, 0) < ATT_DH
    qc = (jnp.where(first, qt, 0.0).astype(BF16), jnp.where(first, 0.0, qt).astype(BF16))
    acc_ref[...] = jnp.zeros(acc_ref.shape, F32)

    def logits(kc, s_ref):
        kj = k_ref[kc]
        for c in range(2):
            s_ref[c] = jnp.dot(kj, qc[c], preferred_element_type=F32)

    def update(kc, s_ref, carry, masked):
        vj = vt_ref[kc]
        out = []
        for c in range(2):
            m_old, l_old = carry[c]
            s = s_ref[c]
            if masked:
                key = lax.broadcasted_iota(jnp.int32, s.shape, 0) + (kc * ATT_BLOCK - qi * tq)
                qry = lax.broadcasted_iota(jnp.int32, s.shape, 1)
                s = jnp.where(key <= qry, s, NEG_INF)
            m_new = jnp.maximum(m_old, jnp.max(s, axis=0, keepdims=True))
            alpha = jnp.exp2(m_old - m_new)
            p = jnp.exp2(s - m_new)
            l_new = alpha * l_old + jnp.sum(p, axis=0, keepdims=True)
            acc_ref[c] = alpha * acc_ref[c] + jnp.dot(vj, p.astype(BF16), preferred_element_type=F32)
            out.append((m_new, l_new))
        return tuple(out)

    last = qi

    def pair(j, carry):
        logits(2 * j + 1, sb_ref)
        carry = update(2 * j, sa_ref, carry, False)
        logits(2 * j + 2, sa_ref)
        return update(2 * j + 1, sb_ref, carry, False)

    logits(0, sa_ref)
    init = ((jnp.full((1, tq), NEG_INF, F32), jnp.zeros((1, tq), F32)),) * 2
    carry = lax.fori_loop(0, last, pair, init)
    logits(2 * last + 1, sb_ref)
    carry = update(2 * last, sa_ref, carry, True)
    (_, l0), (_, l1) = update(2 * last + 1, sb_ref, carry, True)
    o_t = acc_ref[0] / l0 - _lam(lam_ref, lam_init) * (acc_ref[1] / l1)
    o_ref[...] = _rms(o_t.T, g_ref[...]) * (1.0 - lam_init)


def _attn_prompt(q, kb, vtb, lam_p, g, *, lam_init):
    n = q.shape[0]
    nc = n // ATT_BLOCK
    return pl.pallas_call(
        functools.partial(_attn_prompt_kernel, lam_init=lam_init),
        grid=(ATT_HEADS, n // ATT_QBLOCK),
        in_specs=[pl.BlockSpec((ATT_QBLOCK, HEAD_W), lambda h, i: (i, h)),
                  pl.BlockSpec((nc, ATT_BLOCK, HEAD_W), lambda h, i: (0, 0, h)),
                  pl.BlockSpec((nc, HEAD_W, ATT_BLOCK), lambda h, i: (0, h, 0)),
                  pl.BlockSpec((4, ATT_DH), lambda h, i: (0, 0)),
                  pl.BlockSpec((1, HEAD_W), lambda h, i: (0, 0))],
        out_specs=pl.BlockSpec((ATT_QBLOCK, HEAD_W), lambda h, i: (i, h)),
        out_shape=jax.ShapeDtypeStruct((n, ATT_W), F32),
        scratch_shapes=[pltpu.VMEM((2, HEAD_W, ATT_QBLOCK), F32)]
                       + [pltpu.VMEM((2, ATT_BLOCK, ATT_QBLOCK), F32)] * 2,
        compiler_params=_params(("parallel", "arbitrary")),
        name="attn_prompt",
    )(q, kb, vtb, lam_p, g)


def _attn_sample_kernel(pt_ref, q_ref, kn_ref, vn_ref, *rest, lam_init, t_new):
    del pt_ref
    k_refs = rest[:PAGES_PER_STEP]
    v_refs = rest[PAGES_PER_STEP:2 * PAGES_PER_STEP]
    lam_ref, g_ref, o_ref, m_ref, l_ref, acc_ref = rest[2 * PAGES_PER_STEP:]
    step = pl.program_id(1)
    rows = ATT_HEADS * 2 * t_new
    page_cols = PAGE_SIZE * ATT_HEADS

    q = q_ref[...]
    first = lax.broadcasted_iota(jnp.int32, (t_new, HEAD_W), 1) < ATT_DH
    pieces = []
    for h in range(ATT_HEADS):
        qh = q[:, h * HEAD_W:(h + 1) * HEAD_W]
        pieces += [jnp.where(first, qh, 0.0), jnp.where(first, 0.0, qh)]
    qm = jnp.concatenate(pieces, axis=0).astype(BF16)

    def update(s, v):
        m_old = m_ref[...]
        m_new = jnp.maximum(m_old, jnp.max(s, axis=-1, keepdims=True))
        alpha = jnp.exp(m_old - m_new)
        p = jnp.exp(s - m_new)
        l_ref[...] = alpha * l_ref[...] + jnp.sum(p, axis=-1, keepdims=True)
        acc_ref[...] = alpha * acc_ref[...] + jnp.dot(p.astype(BF16), v, preferred_element_type=F32)
        m_ref[...] = m_new

    def logits(keys):
        return lax.dot_general(qm, keys, (((1,), (1,)), ((), ())), preferred_element_type=F32)

    @pl.when(step == 0)
    def _():
        m_ref[...] = jnp.full(m_ref.shape, NEG_INF, F32)
        l_ref[...] = jnp.zeros(l_ref.shape, F32)
        acc_ref[...] = jnp.zeros(acc_ref.shape, F32)
        zpad = jnp.zeros((PAGE_SIZE - ATT_HEADS * t_new, HEAD_W), F32)
        stack = lambda ref: jnp.concatenate(
            [ref[:, h * HEAD_W:(h + 1) * HEAD_W] for h in range(ATT_HEADS)] + [zpad], axis=0).astype(BF16)
        col = lax.broadcasted_iota(jnp.int32, (rows, PAGE_SIZE), 1)
        row = lax.broadcasted_iota(jnp.int32, (rows, PAGE_SIZE), 0)
        valid = (col // t_new == row // (2 * t_new)) & (col % t_new <= row % t_new)
        update(jnp.where(valid, logits(stack(kn_ref)), NEG_INF), stack(vn_ref))

    col = lax.broadcasted_iota(jnp.int32, (rows, page_cols), 1)
    row = lax.broadcasted_iota(jnp.int32, (rows, page_cols), 0)
    bias = jnp.where(col % ATT_HEADS == row // (2 * t_new), 0.0, NEG_INF)
    bias = jnp.concatenate([bias] * PAGES_PER_STEP, axis=1)
    keys = jnp.concatenate([r[...] for r in k_refs], axis=0).astype(BF16)
    vals = jnp.concatenate([r[...] for r in v_refs], axis=0).astype(BF16)
    update(logits(keys) + bias, vals)

    @pl.when(step == pl.num_programs(1) - 1)
    def _():
        lam = _lam(lam_ref, lam_init)
        acc = acc_ref[...]
        l = l_ref[...]
        outs = []
        for h in range(ATT_HEADS):
            r0, r1, r2 = 2 * h * t_new, (2 * h + 1) * t_new, (2 * h + 2) * t_new
            outs.append(_diff_combine(acc[r0:r1], l[r0:r1], acc[r1:r2], l[r1:r2], lam, g_ref[...], lam_init))
        o_ref[...] = jnp.concatenate(outs, axis=-1)


def _attn_sample(q, k_new, v_new, cache_k, cache_v, page_table, lam_p, g, *, layer, lam_init, t_new):
    n = q.shape[0]
    b = n // t_new
    n_pages = page_table.shape[1]
    steps = n_pages // PAGES_PER_STEP
    pool = cache_k.shape[1]
    ck = cache_k.reshape(DEPTH, pool, PAGE_SIZE * ATT_HEADS, HEAD_W)
    cv = cache_v.reshape(DEPTH, pool, PAGE_SIZE * ATT_HEADS, HEAD_W)
    row = pl.BlockSpec((t_new, ATT_W), lambda i, s, pt: (i, 0))

    def page_spec(j):
        return pl.BlockSpec((None, None, PAGE_SIZE * ATT_HEADS, HEAD_W),
                            lambda i, s, pt: (layer, pt[i, s * PAGES_PER_STEP + j], 0, 0))

    pages = [page_spec(j) for j in range(PAGES_PER_STEP)]
    rows = ATT_HEADS * 2 * t_new
    grid_spec = pltpu.PrefetchScalarGridSpec(
        num_scalar_prefetch=1,
        grid=(b, steps),
        in_specs=[row, row, row] + pages + pages + [
            pl.BlockSpec((4, ATT_DH), lambda i, s, pt: (0, 0)),
            pl.BlockSpec((1, HEAD_W), lambda i, s, pt: (0, 0))],
        out_specs=row,
        scratch_shapes=[pltpu.VMEM((rows, 1), F32), pltpu.VMEM((rows, 1), F32),
                        pltpu.VMEM((rows, HEAD_W), F32)],
    )
    return pl.pallas_call(
        functools.partial(_attn_sample_kernel, lam_init=lam_init, t_new=t_new),
        grid_spec=grid_spec,
        out_shape=jax.ShapeDtypeStruct((n, ATT_W), F32),
        compiler_params=_params(("parallel", "arbitrary")),
        name="attn_sample",
    )(page_table, q, k_new, v_new, *([ck] * PAGES_PER_STEP), *([cv] * PAGES_PER_STEP), lam_p, g)


def _rwkv_kernel(z_ref, sh_ref, s0_ref, mu_ref, w0_ref, w2_ref, a0_ref, a2_ref, g2_ref,
                 kk_ref, ka_ref, rk_ref, lng_ref, lnb_ref, seg_ref,
                 c_ref, shn_ref, sn_ref,
                 carry_ref, st_ref, r_s, w_s, k_s, kk_s, b_s, vt_s, yt_s, *, n_valid):
    tt = SCAN_TILE

    @pl.when(pl.program_id(1) == 0)
    def _():
        carry_ref[...] = sh_ref[0]
        st_ref[...] = s0_ref[0]

    z = z_ref[0]
    rowi = lax.broadcasted_iota(jnp.int32, z.shape, 0)
    prev = jnp.where(rowi == 0, carry_ref[...], pltpu.roll(z, 1, 0))
    shn_ref[0] = z[n_valid - 1:n_valid, :]
    carry_ref[...] = z[tt - 1:tt, :]
    zs = z + (prev - z) * mu_ref[...]

    def segsum(x):
        return jnp.dot(x, seg_ref[...], precision=HI, preferred_element_type=F32)

    r = zs[:, 0:RWKV_W]
    k = zs[:, RWKV_W:2 * RWKV_W]
    v = zs[:, 2 * RWKV_W:3 * RWKV_W]
    wa = zs[:, 3 * RWKV_W:3 * RWKV_W + W_LORA + A_LORA]
    gd = zs[:, 3 * RWKV_W + W_LORA + A_LORA:]
    wx = w0_ref[...] + jnp.dot(jnp.tanh(wa), w2_ref[...], precision=HI, preferred_element_type=F32)
    softplus = jnp.maximum(-wx, 0.0) + jnp.log1p(jnp.exp(-jnp.abs(wx)))
    w_log = -softplus - 0.5
    decay = jnp.exp(-jnp.exp(w_log))
    a = jax.nn.sigmoid(a0_ref[...] + jnp.dot(wa, a2_ref[...], precision=HI, preferred_element_type=F32))
    g = jnp.dot(jax.nn.sigmoid(gd), g2_ref[...], precision=HI, preferred_element_type=F32)
    kk = k * kk_ref[...]
    kk = kk * lax.rsqrt(segsum(kk * kk) + 1e-12)
    k = k * (1.0 + (a - 1.0) * ka_ref[...])
    bonus = segsum(r * k * rk_ref[...]) * v

    r_s[...] = r
    w_s[...] = decay
    k_s[...] = k
    kk_s[...] = kk
    b_s[...] = kk * a
    vt_s[...] = v.T
    yt_s[...] = jnp.zeros(yt_s.shape, F32)

    lane = lax.broadcasted_iota(jnp.int32, (RWKV_DH, 2 * RWKV_DH), 1)
    lo = lane < RWKV_DH

    def pair_sum(x):
        s_lo = jnp.sum(jnp.where(lo, x, 0.0), axis=-1, keepdims=True)
        s_hi = jnp.sum(jnp.where(lo, 0.0, x), axis=-1, keepdims=True)
        return s_lo, s_hi

    def step(t, rows8, i, states):
        hot = lane == t
        new_states = []
        for p in range(2):
            sl = slice(p * 2 * RWKV_DH, (p + 1) * 2 * RWKV_DH)
            row_t = lambda name: rows8[name][i:i + 1, sl]
            s_old = states[p]
            sa_lo, sa_hi = pair_sum(s_old * row_t("kk"))
            h0 = slice(2 * p * RWKV_DH, (2 * p + 1) * RWKV_DH)
            h1 = slice((2 * p + 1) * RWKV_DH, (2 * p + 2) * RWKV_DH)
            v_lo = jnp.sum(jnp.where(hot, vt_s[h0, :], 0.0), axis=-1, keepdims=True)
            v_hi = jnp.sum(jnp.where(hot, vt_s[h1, :], 0.0), axis=-1, keepdims=True)
            s_new = (s_old * row_t("w") - jnp.where(lo, sa_lo, sa_hi) * row_t("b")
                     + jnp.where(lo, v_lo, v_hi) * row_t("k"))
            y_lo, y_hi = pair_sum(s_new * row_t("r"))
            yt_s[h0, :] = jnp.where(hot, y_lo, yt_s[h0, :])
            yt_s[h1, :] = jnp.where(hot, y_hi, yt_s[h1, :])
            new_states.append(s_new)
        return tuple(new_states)

    sub = 8

    def group(gi, states):
        base = pl.multiple_of(gi * sub, sub)
        rows8 = {name: ref[pl.ds(base, sub), :]
                 for name, ref in (("r", r_s), ("w", w_s), ("k", k_s), ("kk", kk_s), ("b", b_s))}
        for i in range(sub):
            states = step(base + i, rows8, i, states)
        return states

    final = lax.fori_loop(0, n_valid // sub, group, (st_ref[0], st_ref[1]))
    st_ref[0] = final[0]
    st_ref[1] = final[1]
    sn_ref[0] = st_ref[...]

    y = yt_s[...].T
    mean = segsum(y) * (1.0 / RWKV_DH)
    d = y - mean
    var = segsum(d * d) * (1.0 / RWKV_DH)
    y = d * lax.rsqrt(var + RWKV_GN_EPS) * lng_ref[...] + lnb_ref[...]
    c_ref[0] = (y + bonus) * g


def _rwkv(z, shift, state, mu, w0, w2p, a0, a2p, g2, kk, ka, rk, lng, lnb, seg, *, n_valid):
    b, t, _ = z.shape
    tt = SCAN_TILE
    vec = lambda w: _const_spec((1, w))
    return pl.pallas_call(
        functools.partial(_rwkv_kernel, n_valid=n_valid),
        grid=(b, t // tt),
        in_specs=[pl.BlockSpec((1, tt, RWKV_IN), lambda i, j: (i, j, 0)),
                  pl.BlockSpec((1, 1, RWKV_IN), lambda i, j: (i, 0, 0)),
                  pl.BlockSpec((1, 2, RWKV_DH, 2 * RWKV_DH), lambda i, j: (i, 0, 0, 0)),
                  vec(RWKV_IN), vec(RWKV_W), _const_spec((W_LORA + A_LORA, RWKV_W)), vec(RWKV_W),
                  _const_spec((W_LORA + A_LORA, RWKV_W)), _const_spec((G_LORA, RWKV_W)),
                  vec(RWKV_W), vec(RWKV_W), vec(RWKV_W), vec(RWKV_W), vec(RWKV_W),
                  _const_spec((RWKV_W, RWKV_W))],
        out_specs=[pl.BlockSpec((1, tt, RWKV_W), lambda i, j: (i, j, 0)),
                   pl.BlockSpec((1, 1, RWKV_IN), lambda i, j: (i, 0, 0)),
                   pl.BlockSpec((1, 2, RWKV_DH, 2 * RWKV_DH), lambda i, j: (i, 0, 0, 0))],
        out_shape=[jax.ShapeDtypeStruct((b, t, RWKV_W), F32),
                   jax.ShapeDtypeStruct((b, 1, RWKV_IN), F32),
                   jax.ShapeDtypeStruct((b, 2, RWKV_DH, 2 * RWKV_DH), F32)],
        scratch_shapes=[pltpu.VMEM((1, RWKV_IN), F32), pltpu.VMEM((2, RWKV_DH, 2 * RWKV_DH), F32)]
                       + [pltpu.VMEM((tt, RWKV_W), F32)] * 5
                       + [pltpu.VMEM((RWKV_W, tt), F32)] * 2,
        compiler_params=_params(("parallel", "arbitrary")),
        name="rwkv7",
    )(z, shift, state, mu, w0, w2p, a0, a2p, g2, kk, ka, rk, lng, lnb, seg)


RW_CHUNK = 64
RW_TILE = 256
SUBLANES = 8
NN = (((1,), (0,)), ((), ()))
NT = (((1,), (1,)), ((), ()))


def _split(a):
    hi = a.astype(BF16)
    return hi, (a - hi.astype(F32)).astype(BF16)


def _dot3s(a, b, dims=NN):
    d = lambda x, y: lax.dot_general(x, y, dims, preferred_element_type=F32)
    return d(a[0], b[0]) + (d(a[0], b[1]) + d(a[1], b[0]))


def _dot3(a, b, dims=NN):
    return _dot3s(_split(a), _split(b), dims)


def _rwkv_chunk_kernel(z_ref, sh_ref, s0_ref, mu_ref, w0_ref, w2_ref, a0_ref, a2_ref, g2_ref,
                       kk_ref, ka_ref, rk_ref, lng_ref, lnb_ref, seg_ref,
                       c_ref, shn_ref, sn_ref, carry_ref, st_ref, y_s):
    tt, cs = RW_TILE, RW_CHUNK

    @pl.when(pl.program_id(1) == 0)
    def _():
        carry_ref[...] = sh_ref[0]
        st_ref[...] = s0_ref[0]

    z = z_ref[0]
    rowi = lax.broadcasted_iota(jnp.int32, z.shape, 0)
    prev = jnp.where(rowi == 0, carry_ref[...], pltpu.roll(z, 1, 0))
    shn_ref[0] = z[tt - 1:tt, :]
    carry_ref[...] = z[tt - 1:tt, :]
    zs = z + (prev - z) * mu_ref[...]

    seg = _split(seg_ref[...])

    def segsum(x):
        return _dot3s(_split(x), seg)

    r = zs[:, 0:RWKV_W]
    k = zs[:, RWKV_W:2 * RWKV_W]
    v = zs[:, 2 * RWKV_W:3 * RWKV_W]
    wa = zs[:, 3 * RWKV_W:3 * RWKV_W + W_LORA + A_LORA]
    gd = zs[:, 3 * RWKV_W + W_LORA + A_LORA:]
    wx = w0_ref[...] + _dot3(jnp.tanh(wa), w2_ref[...])
    softplus = jnp.maximum(-wx, 0.0) + jnp.log1p(jnp.exp(-jnp.abs(wx)))
    logw = -jnp.exp(-softplus - 0.5)
    a = jax.nn.sigmoid(a0_ref[...] + _dot3(wa, a2_ref[...]))
    g = _dot3(jax.nn.sigmoid(gd), g2_ref[...])
    kk = k * kk_ref[...]
    kk = kk * lax.rsqrt(segsum(kk * kk) + 1e-12)
    k = k * (1.0 + (a - 1.0) * ka_ref[...])
    bonus = segsum(r * k * rk_ref[...]) * v
    b = kk * a

    lane_head = lax.broadcasted_iota(jnp.int32, (1, RWKV_W), 1) // RWKV_DH
    head_mask = [lane_head == h for h in range(RWKV_HEADS)]
    row = lax.broadcasted_iota(jnp.int32, (2 * cs, cs), 0)
    col = lax.broadcasted_iota(jnp.int32, (2 * cs, cs), 1)
    tril2 = col <= jnp.where(row < cs, row - 1, row - cs)
    tri_incl = _split((lax.broadcasted_iota(jnp.int32, (cs, cs), 0)
                       >= lax.broadcasted_iota(jnp.int32, (cs, cs), 1)).astype(F32))
    bd_row = lax.broadcasted_iota(jnp.int32, (RWKV_W, RWKV_W), 0) // RWKV_DH
    bd_col = lax.broadcasted_iota(jnp.int32, (RWKV_W, RWKV_W), 1) // RWKV_DH
    block_diag = bd_row == bd_col
    lo_half = lax.broadcasted_iota(jnp.int32, (SUBLANES, 2 * RWKV_DH), 1) < RWKV_DH
    n_sub = cs // SUBLANES

    for ci in range(tt // cs):
        sl = slice(ci * cs, (ci + 1) * cs)
        r_c, lw_c, k_c, v_c, kk_c, b_c = r[sl], logw[sl], k[sl], v[sl], kk[sl], b[sl]
        cum = _dot3s(tri_incl, _split(lw_c))
        total = cum[cs - 1:cs, :]
        inv = jnp.exp(-cum)
        fin = jnp.exp(total - cum)
        lhs = _split(jnp.concatenate([kk_c * jnp.exp(cum - lw_c), r_c * jnp.exp(cum)], axis=0))
        kh, bh = _split(k_c * inv), _split(b_c * inv)
        v_s = _split(v_c)

        wy = jnp.zeros((2 * cs, RWKV_W), F32)
        l_mats, yb_mats = [], []
        for h in range(RWKV_HEADS):
            lhs_h = (jnp.where(head_mask[h], lhs[0], 0), jnp.where(head_mask[h], lhs[1], 0))
            gk = jnp.where(tril2, _dot3s(lhs_h, kh, NT), 0.0)
            gb = jnp.where(tril2, _dot3s(lhs_h, bh, NT), 0.0)
            wy = wy + jnp.where(head_mask[h], _dot3s(_split(gk), v_s), 0.0)
            l_mats.append(gb[:cs])
            yb_mats.append(gb[cs:])

        state = st_ref[...]
        sm = _dot3s(lhs, _split(state), NT)
        rhs = sm[:cs] + wy[:cs]
        tiles = [rhs[i * SUBLANES:(i + 1) * SUBLANES] for i in range(n_sub)]
        for j in range(cs):
            i0 = j // SUBLANES
            u_j = tiles[i0][j % SUBLANES:j % SUBLANES + 1, :]
            for i in range(i0, n_sub):
                rows = slice(i * SUBLANES, (i + 1) * SUBLANES)
                bc = lambda h: jnp.broadcast_to(l_mats[h][rows, j:j + 1], (SUBLANES, 2 * RWKV_DH))
                l_col = jnp.concatenate([jnp.where(lo_half, bc(0), bc(1)),
                                         jnp.where(lo_half, bc(2), bc(3))], axis=1)
                tiles[i] = tiles[i] - l_col * u_j
        u = jnp.concatenate(tiles, axis=0)
        u_s = _split(u)
        bu = jnp.zeros((cs, RWKV_W), F32)
        for h in range(RWKV_HEADS):
            bu = bu + jnp.where(head_mask[h], _dot3s(_split(yb_mats[h]), u_s), 0.0)
        y_s[sl, :] = sm[cs:] + wy[cs:] - bu
        delta = _dot3s(_split(v_c.T), _split(k_c * fin)) - _dot3s(_split(u.T), _split(b_c * fin))
        st_ref[...] = state * jnp.exp(total) + jnp.where(block_diag, delta, 0.0)

    sn_ref[0] = st_ref[...]
    y = y_s[...]
    mean = segsum(y) * (1.0 / RWKV_DH)
    d = y - mean
    var = segsum(d * d) * (1.0 / RWKV_DH)
    y = d * lax.rsqrt(var + RWKV_GN_EPS) * lng_ref[...] + lnb_ref[...]
    c_ref[0] = (y + bonus) * g


def _rwkv_chunked(z, shift, state, mu, w0, w2p, a0, a2p, g2, kk, ka, rk, lng, lnb, seg):
    b, t, _ = z.shape
    tt = RW_TILE
    vec = lambda w: _const_spec((1, w))
    return pl.pallas_call(
        _rwkv_chunk_kernel,
        grid=(b, t // tt),
        in_specs=[pl.BlockSpec((1, tt, RWKV_IN), lambda i, j: (i, j, 0)),
                  pl.BlockSpec((1, 1, RWKV_IN), lambda i, j: (i, 0, 0)),
                  pl.BlockSpec((1, RWKV_W, RWKV_W), lambda i, j: (i, 0, 0)),
                  vec(RWKV_IN), vec(RWKV_W), _const_spec((W_LORA + A_LORA, RWKV_W)), vec(RWKV_W),
                  _const_spec((W_LORA + A_LORA, RWKV_W)), _const_spec((G_LORA, RWKV_W)),
                  vec(RWKV_W), vec(RWKV_W), vec(RWKV_W), vec(RWKV_W), vec(RWKV_W),
                  _const_spec((RWKV_W, RWKV_W))],
        out_specs=[pl.BlockSpec((1, tt, RWKV_W), lambda i, j: (i, j, 0)),
                   pl.BlockSpec((1, 1, RWKV_IN), lambda i, j: (i, 0, 0)),
                   pl.BlockSpec((1, RWKV_W, RWKV_W), lambda i, j: (i, 0, 0))],
        out_shape=[jax.ShapeDtypeStruct((b, t, RWKV_W), F32),
                   jax.ShapeDtypeStruct((b, 1, RWKV_IN), F32),
                   jax.ShapeDtypeStruct((b, RWKV_W, RWKV_W), F32)],
        scratch_shapes=[pltpu.VMEM((1, RWKV_IN), F32), pltpu.VMEM((RWKV_W, RWKV_W), F32),
                        pltpu.VMEM((tt, RWKV_W), F32)],
        compiler_params=_params(("parallel", "arbitrary")),
        name="rwkv7_chunked",
    )(z, shift, state, mu, w0, w2p, a0, a2p, g2, kk, ka, rk, lng, lnb, seg)


def _block_diag_state(s):
    b = s.shape[0]
    eye = jnp.eye(RWKV_HEADS, dtype=s.dtype)
    return jnp.einsum("bhvk,hg->bhvgk", s, eye).reshape(b, RWKV_W, RWKV_W)


def _diag_blocks(s):
    b = s.shape[0]
    s = s.reshape(b, RWKV_HEADS, RWKV_DH, RWKV_HEADS, RWKV_DH)
    return jnp.stack([s[:, h, :, h, :] for h in range(RWKV_HEADS)], axis=1)


def _pair_state(s):
    b = s.shape[0]
    s = s.reshape(b, 2, 2, RWKV_DH, RWKV_DH)
    return jnp.transpose(s, (0, 1, 3, 2, 4)).reshape(b, 2, RWKV_DH, 2 * RWKV_DH)


def _unpair_state(s):
    b = s.shape[0]
    s = s.reshape(b, 2, RWKV_DH, 2, RWKV_DH)
    return jnp.transpose(s, (0, 1, 3, 2, 4)).reshape(b, RWKV_HEADS, RWKV_DH, RWKV_DH)


def _out_ffn_kernel(x_ref, a_ref, b_ref, c_ref, gs_ref, wo_ref, g2_ref, wg_ref, wu_ref, wd_ref,
                    fg_ref, o_ref, acc_ref, *, final):
    mix = jnp.concatenate([a_ref[...], b_ref[...], c_ref[...]], axis=-1) * gs_ref[...]
    x1 = x_ref[...] + jnp.dot(mix.astype(BF16), wo_ref[...], preferred_element_type=F32)
    acc_ref[...] = x1
    h2 = _rms(x1, g2_ref[...]).astype(BF16)
    for c in range(0, D_FF, FF_CHUNK):
        gate = jnp.dot(h2, wg_ref[:, c:c + FF_CHUNK], preferred_element_type=F32)
        up = jnp.dot(h2, wu_ref[:, c:c + FF_CHUNK], preferred_element_type=F32)
        act = (gate * jax.nn.sigmoid(gate) * up).astype(BF16)
        acc_ref[...] += jnp.dot(act, wd_ref[c:c + FF_CHUNK, :], preferred_element_type=F32)
    x2 = acc_ref[...]
    o_ref[...] = _rms(x2, fg_ref[...]) if final else x2


def _out_ffn(x, a, b, c, gs, wo, g2, wg, wu, wd, fg, *, tm, final):
    n = x.shape[0]
    row = lambda w: pl.BlockSpec((tm, w), lambda i: (i, 0))
    return pl.pallas_call(
        functools.partial(_out_ffn_kernel, final=final),
        grid=(n // tm,),
        in_specs=[row(D_MODEL), row(CONV_W), row(ATT_W), row(RWKV_W), _const_spec((1, D_MODEL)),
                  _const_spec((D_MODEL, D_MODEL)), _const_spec((1, D_MODEL)),
                  _const_spec((D_MODEL, D_FF)), _const_spec((D_MODEL, D_FF)),
                  _const_spec((D_FF, D_MODEL)), _const_spec((1, D_MODEL))],
        out_specs=row(D_MODEL),
        out_shape=jax.ShapeDtypeStruct((n, D_MODEL), F32),
        scratch_shapes=[pltpu.VMEM((tm, D_MODEL), F32)],
        compiler_params=_params(("parallel",)),
        name="out_ffn",
    )(x, a, b, c, gs, wo, g2, wg, wu, wd, fg)


def _rope_tables(pos):
    half = ATT_DH // 2
    inv = ROPE_THETA ** (-jnp.arange(half, dtype=F32) / half)
    ang = pos.astype(F32)[:, None] * inv[None, :]
    tile = lambda t: jnp.concatenate([t] * (HEAD_W // half), axis=-1)
    return tile(jnp.cos(ang)), tile(jnp.sin(ang))


def _rotate_half_columns(w):
    half = ATT_DH // 2
    w = w.reshape(w.shape[0], ATT_W // ATT_DH, 2, half)
    return jnp.concatenate([-w[:, :, 1], w[:, :, 0]], axis=-1).reshape(w.shape[0], ATT_W)


def kernel(x_prompt, x_sample, cache_k, cache_v, state_conv, state_shift, state_wkv, page_table, norm1_g, w_in, conv_w, conv_b, conv_ln_g, conv_ln_b, conv_pw_w, conv_pw_b, att_lambda, att_subln_g, rwkv_mu, rwkv_w0, rwkv_w2, rwkv_a0, rwkv_a2, rwkv_g2, rwkv_kk, rwkv_ka, rwkv_rk, rwkv_ln_g, rwkv_ln_b, group_scale, w_out, norm2_g, w_gate, w_up, w_down, final_g):
    bp, tp, _ = x_prompt.shape
    bs, ts, _ = x_sample.shape
    assert bp == 1 and tp % ATT_QBLOCK == 0 and ts <= SCAN_TILE and ts % 8 == 0
    past_len = page_table.shape[1] * PAGE_SIZE
    groups = {
        "p": dict(b=bp, t=tp, tm=512, conv_tt=256, x=x_prompt.reshape(bp * tp, D_MODEL),
                  pos=jnp.arange(tp, dtype=jnp.int32)),
        "s": dict(b=bs, t=ts, tm=bs * ts, conv_tt=ts, x=x_sample.reshape(bs * ts, D_MODEL),
                  pos=jnp.tile(past_len + jnp.arange(ts, dtype=jnp.int32), bs)),
    }
    for grp in groups.values():
        grp["cos"], grp["sin"] = _rope_tables(grp["pos"])
    outs = {name: dict(k=[], v=[], conv=[], shift=[], wkv=[]) for name in groups}

    seg = jnp.asarray(np.kron(np.eye(RWKV_HEADS, dtype=np.float32),
                              np.ones((RWKV_DH, RWKV_DH), np.float32)))
    row = lambda p: p.reshape(1, -1)

    for l in range(DEPTH):
        lam_init = 0.8 - 0.6 * math.exp(-0.3 * l)
        att_w = w_in[l][:, 2 * CONV_W:2 * CONV_W + 2 * ATT_W]
        w_ext = jnp.concatenate([w_in[l], _rotate_half_columns(att_w[:, :ATT_W]),
                                 _rotate_half_columns(att_w[:, ATT_W:])], axis=-1).astype(BF16)
        w2p = jnp.concatenate([rwkv_w2[l], jnp.zeros((A_LORA, RWKV_W), F32)], axis=0)
        a2p = jnp.concatenate([jnp.zeros((W_LORA, RWKV_W), F32), rwkv_a2[l]], axis=0)
        subln = row(att_subln_g[l])
        for name, grp in groups.items():
            b, t = grp["b"], grp["t"]
            prompt = name == "p"
            zc, q, k, v, zr, *kv_bf = _in_proj(grp["x"], row(norm1_g[l]), w_ext, grp["cos"], grp["sin"],
                                               tm=grp["tm"], emit_bf16=prompt)
            if prompt:
                conv_state = jnp.zeros((b, CONV_HALO, CONV_W), F32)
            else:
                conv_state = jnp.pad(state_conv[l], ((0, 0), (CONV_HALO - (CONV_K - 1), 0), (0, 0)))
            a_out, conv_new = _conv(zc.reshape(b, t, 2 * CONV_W), conv_state, conv_w[l].reshape(CONV_K, CONV_W),
                                    row(conv_b[l]), row(conv_ln_g[l]), row(conv_ln_b[l]),
                                    conv_pw_w[l].astype(BF16), row(conv_pw_b[l]), tt=grp["conv_tt"])
            if prompt:
                b_out = _attn_prompt(q, kv_bf[0], kv_bf[1], att_lambda[l], subln, lam_init=lam_init)
            else:
                b_out = _attn_sample(q, k, v, cache_k, cache_v, page_table, att_lambda[l], subln,
                                     layer=l, lam_init=lam_init, t_new=t)
            zr3 = zr.reshape(b, t, RWKV_IN)
            rw_params = (row(rwkv_mu[l]), row(rwkv_w0[l]), w2p, row(rwkv_a0[l]), a2p, rwkv_g2[l],
                         row(rwkv_kk[l]), row(rwkv_ka[l]), row(rwkv_rk[l]), row(rwkv_ln_g[l]),
                         row(rwkv_ln_b[l]), seg)
            if prompt:
                c_out, shift_new, wkv_new = _rwkv_chunked(
                    zr3, jnp.zeros((b, 1, RWKV_IN), F32), jnp.zeros((b, RWKV_W, RWKV_W), F32), *rw_params)
                wkv_new = _diag_blocks(wkv_new)
            else:
                zr3 = jnp.pad(zr3, ((0, 0), (0, SCAN_TILE - t), (0, 0)))
                c_out, shift_new, wkv_new = _rwkv(
                    zr3, state_shift[l].reshape(b, 1, RWKV_IN), _pair_state(state_wkv[l]), *rw_params,
                    n_valid=t)
                wkv_new = _unpair_state(wkv_new)
            c_out = c_out[:, :t].reshape(b * t, RWKV_W)
            grp["x"] = _out_ffn(grp["x"], a_out.reshape(b * t, CONV_W), b_out, c_out, row(group_scale[l]),
                                w_out[l].astype(BF16), row(norm2_g[l]), w_gate[l].astype(BF16),
                                w_up[l].astype(BF16), w_down[l].astype(BF16), row(final_g),
                                tm=grp["tm"], final=(l == DEPTH - 1))
            o = outs[name]
            o["k"].append(k.reshape(b, t, ATT_HEADS, HEAD_W))
            o["v"].append(v.reshape(b, t, ATT_HEADS, HEAD_W))
            o["conv"].append(conv_new[:, CONV_HALO - (CONV_K - 1):])
            o["shift"].append(shift_new.reshape(b, RWKV_IN))
            o["wkv"].append(wkv_new)

    y_prompt = groups["p"]["x"].reshape(bp, tp, D_MODEL)
    y_sample = groups["s"]["x"].reshape(bs, ts, D_MODEL)
    st = lambda name, key: jnp.stack(outs[name][key])
    return (y_prompt, y_sample,
            st("p", "k"), st("p", "v"), st("p", "conv"), st("p", "shift"), st("p", "wkv"),
            st("s", "k"), st("s", "v"), st("s", "conv"), st("s", "shift"), st("s", "wkv"))
```

```python
import functools
import math

import jax
import jax.numpy as jnp
import numpy as np
from jax import lax
from jax.experimental import pallas as pl
from jax.experimental.pallas import tpu as pltpu

D_MODEL = 1024
DEPTH = 2
PAGE_SIZE = 128
CONV_W = 256
ATT_W = 512
RWKV_W = 256
ATT_HEADS = 4
ATT_DH = 64
HEAD_W = 2 * ATT_DH
RWKV_DH = 64
RWKV_HEADS = 4
CONV_K = 31
W_LORA = 64
A_LORA = 64
G_LORA = 128
RWKV_IN = 3 * RWKV_W + W_LORA + A_LORA + G_LORA
IN_W = 2 * CONV_W + 3 * ATT_W + RWKV_IN
D_FF = 2816
ROPE_THETA = 10000.0
NORM_EPS = 1e-6
LN_EPS = 1e-5
RWKV_GN_EPS = 64e-5
NEG_INF = -1e30

F32 = jnp.float32
BF16 = jnp.bfloat16
HI = lax.Precision.HIGHEST

CONV_HALO = 32
CONV_ROWS = 64
FF_CHUNK = 256
SCAN_TILE = 128
ATT_BLOCK = 512
ATT_QBLOCK = 2 * ATT_BLOCK
LOG2E = math.log2(math.e)
PAGES_PER_STEP = 32
VMEM_LIMIT = 56 * 1024 * 1024


def _const_spec(shape):
    zeros = (0,) * len(shape)
    return pl.BlockSpec(shape, lambda *_: zeros, pipeline_mode=pl.Buffered(1))


def _params(sem):
    return pltpu.CompilerParams(dimension_semantics=sem, vmem_limit_bytes=VMEM_LIMIT)


def _rms(x, g):
    return x * lax.rsqrt(jnp.mean(x * x, axis=-1, keepdims=True) + NORM_EPS) * g


def _in_proj_kernel(x_ref, g_ref, w_ref, cos_ref, sin_ref, *out_refs, emit_bf16):
    if emit_bf16:
        zc_ref, q_ref, k_ref, v_ref, zr_ref, kb_ref, vb_ref = out_refs
    else:
        zc_ref, q_ref, k_ref, v_ref, zr_ref = out_refs
    h = _rms(x_ref[...], g_ref[...]).astype(BF16)

    def mm(c0, c1):
        return jnp.dot(h, w_ref[:, c0:c1], preferred_element_type=F32)

    cos = jnp.concatenate([cos_ref[...]] * ATT_HEADS, axis=-1)
    sin = jnp.concatenate([sin_ref[...]] * ATT_HEADS, axis=-1)
    q0, k0, v0, r0 = 2 * CONV_W, 2 * CONV_W + ATT_W, 2 * CONV_W + 2 * ATT_W, 2 * CONV_W + 3 * ATT_W
    zc_ref[...] = mm(0, q0)
    q = mm(q0, k0) * cos + mm(IN_W, IN_W + ATT_W) * sin
    q_ref[...] = q * (ATT_DH ** -0.5)
    k = mm(k0, v0) * cos + mm(IN_W + ATT_W, IN_W + 2 * ATT_W) * sin
    v = mm(v0, r0)
    if emit_bf16:
        tm = k.shape[0]
        for head in range(ATT_HEADS):
            lanes = slice(head * HEAD_W, (head + 1) * HEAD_W)
            k_ref[pl.ds(head, tm, stride=ATT_HEADS), :] = k[:, lanes]
            v_ref[pl.ds(head, tm, stride=ATT_HEADS), :] = v[:, lanes]
    else:
        k_ref[...] = k
        v_ref[...] = v
    zr_ref[...] = mm(r0, IN_W)
    if emit_bf16:
        kb_ref[0] = k.astype(BF16)
        vb_ref[0] = v.T.astype(BF16)


def _in_proj(x, g, w_ext, cos, sin, *, tm, emit_bf16):
    n = x.shape[0]
    row = lambda w: pl.BlockSpec((tm, w), lambda i: (i, 0))
    widths = [2 * CONV_W, ATT_W, ATT_W, ATT_W, RWKV_IN]
    out_shape = [jax.ShapeDtypeStruct((n, w), F32) for w in widths]
    out_specs = [row(w) for w in widths]
    if emit_bf16:
        assert tm == ATT_BLOCK
        for i in (2, 3):
            out_shape[i] = jax.ShapeDtypeStruct((n * ATT_HEADS, HEAD_W), F32)
            out_specs[i] = pl.BlockSpec((tm * ATT_HEADS, HEAD_W), lambda i: (i, 0))
        out_shape += [jax.ShapeDtypeStruct((n // tm, tm, ATT_W), BF16),
                      jax.ShapeDtypeStruct((n // tm, ATT_W, tm), BF16)]
        out_specs += [pl.BlockSpec((1, tm, ATT_W), lambda i: (i, 0, 0)),
                      pl.BlockSpec((1, ATT_W, tm), lambda i: (i, 0, 0))]
    return pl.pallas_call(
        functools.partial(_in_proj_kernel, emit_bf16=emit_bf16),
        grid=(n // tm,),
        in_specs=[row(D_MODEL), _const_spec((1, D_MODEL)), _const_spec(w_ext.shape),
                  row(HEAD_W), row(HEAD_W)],
        out_specs=out_specs,
        out_shape=out_shape,
        compiler_params=_params(("parallel",)),
        name="in_proj",
    )(x, g, w_ext, cos, sin)


def _conv_kernel(zc_ref, st_ref, cw_ref, cb_ref, lg_ref, lb_ref, pw_ref, pb_ref,
                 a_ref, cn_ref, pad_ref, *, tt):
    @pl.when(pl.program_id(1) == 0)
    def _():
        pad_ref[0:CONV_HALO, :] = st_ref[0]

    z = zc_ref[0]
    pad_ref[CONV_HALO:CONV_HALO + tt, :] = z[:, :CONV_W] * jax.nn.sigmoid(z[:, CONV_W:])
    rc = min(tt, CONV_ROWS)
    first = CONV_HALO - (CONV_K - 1)
    for r0 in range(0, tt, rc):
        acc = jnp.zeros((rc, CONV_W), F32)
        for j in range(CONV_K):
            acc = acc + cw_ref[j:j + 1, :] * pad_ref[r0 + first + j:r0 + first + j + rc, :]
        y = acc + cb_ref[...]
        mu = jnp.mean(y, axis=-1, keepdims=True)
        d = y - mu
        var = jnp.mean(d * d, axis=-1, keepdims=True)
        y = d * lax.rsqrt(var + LN_EPS) * lg_ref[...] + lb_ref[...]
        y = (y * jax.nn.sigmoid(y)).astype(BF16)
        a_ref[0, r0:r0 + rc, :] = jnp.dot(y, pw_ref[...], preferred_element_type=F32) + pb_ref[...]
    tail = pad_ref[tt:tt + CONV_HALO, :]
    cn_ref[0] = tail
    pad_ref[0:CONV_HALO, :] = tail


def _conv(zc, state, cw, cb, lg, lb, pw, pb, *, tt):
    b, t, _ = zc.shape
    return pl.pallas_call(
        functools.partial(_conv_kernel, tt=tt),
        grid=(b, t // tt),
        in_specs=[pl.BlockSpec((1, tt, 2 * CONV_W), lambda i, j: (i, j, 0)),
                  pl.BlockSpec((1, CONV_HALO, CONV_W), lambda i, j: (i, 0, 0)),
                  _const_spec((CONV_K, CONV_W)), _const_spec((1, CONV_W)), _const_spec((1, CONV_W)),
                  _const_spec((1, CONV_W)), _const_spec((CONV_W, CONV_W)), _const_spec((1, CONV_W))],
        out_specs=[pl.BlockSpec((1, tt, CONV_W), lambda i, j: (i, j, 0)),
                   pl.BlockSpec((1, CONV_HALO, CONV_W), lambda i, j: (i, 0, 0))],
        out_shape=[jax.ShapeDtypeStruct((b, t, CONV_W), F32),
                   jax.ShapeDtypeStruct((b, CONV_HALO, CONV_W), F32)],
        scratch_shapes=[pltpu.VMEM((CONV_HALO + tt, CONV_W), F32)],
        compiler_params=_params(("parallel", "arbitrary")),
        name="conv_module",
    )(zc, state, cw, cb, lg, lb, pw, pb)


def _lam(lam_ref, lam_init):
    lp = lam_ref[...]
    s1 = jnp.sum(lp[0:1] * lp[1:2], axis=-1, keepdims=True)
    s2 = jnp.sum(lp[2:3] * lp[3:4], axis=-1, keepdims=True)
    return jnp.exp(s1) - jnp.exp(s2) + lam_init


def _diff_combine(acc0, l0, acc1, l1, lam, g, lam_init):
    o = acc0 / l0 - lam * (acc1 / l1)
    return _rms(o, g) * (1.0 - lam_init)


def _attn_prompt_kernel(q_ref, k_ref, vt_ref, lam_ref, g_ref, o_ref, acc_ref, sa_ref, sb_ref, *, lam_init):
    qi = pl.program_id(1)
    tq = ATT_QBLOCK
    qt = (q_ref[...] * LOG2E).T
    first = lax.broadcasted_iota(jnp.int32, qt.shape, 0) < ATT_DH
    qc = (jnp.where(first, qt, 0.0).astype(BF16), jnp.where(first, 0.0, qt).astype(BF16))
    acc_ref[...] = jnp.zeros(acc_ref.shape, F32)

    def logits(kc, s_ref):
        kj = k_ref[kc]
        for c in range(2):
            s_ref[c] = jnp.dot(kj, qc[c], preferred_element_type=F32)

    def update(kc, s_ref, carry, masked):
        vj = vt_ref[kc]
        out = []
        for c in range(2):
            m_old, l_old = carry[c]
            s = s_ref[c]
            if masked:
                key = lax.broadcasted_iota(jnp.int32, s.shape, 0) + (kc * ATT_BLOCK - qi * tq)
                qry = lax.broadcasted_iota(jnp.int32, s.shape, 1)
                s = jnp.where(key <= qry, s, NEG_INF)
            m_new = jnp.maximum(m_old, jnp.max(s, axis=0, keepdims=True))
            alpha = jnp.exp2(m_old - m_new)
            p = jnp.exp2(s - m_new)
            l_new = alpha * l_old + jnp.sum(p, axis=0, keepdims=True)
            acc_ref[c] = alpha * acc_ref[c] + jnp.dot(vj, p.astype(BF16), preferred_element_type=F32)
            out.append((m_new, l_new))
        return tuple(out)

    last = qi

    def pair(j, carry):
        logits(2 * j + 1, sb_ref)
        carry = update(2 * j, sa_ref, carry, False)
        logits(2 * j + 2, sa_ref)
        return update(2 * j + 1, sb_ref, carry, False)

    logits(0, sa_ref)
    init = ((jnp.full((1, tq), NEG_INF, F32), jnp.zeros((1, tq), F32)),) * 2
    carry = lax.fori_loop(0, last, pair, init)
    logits(2 * last + 1, sb_ref)
    carry = update(2 * last, sa_ref, carry, True)
    (_, l0), (_, l1) = update(2 * last + 1, sb_ref, carry, True)
    o_t = acc_ref[0] / l0 - _lam(lam_ref, lam_init) * (acc_ref[1] / l1)
    o_ref[...] = _rms(o_t.T, g_ref[...]) * (1.0 - lam_init)


def _attn_prompt(q, kb, vtb, lam_p, g, *, lam_init):
    n = q.shape[0]
    nc = n // ATT_BLOCK
    return pl.pallas_call(
        functools.partial(_attn_prompt_kernel, lam_init=lam_init),
        grid=(ATT_HEADS, n // ATT_QBLOCK),
        in_specs=[pl.BlockSpec((ATT_QBLOCK, HEAD_W), lambda h, i: (i, h)),
                  pl.BlockSpec((nc, ATT_BLOCK, HEAD_W), lambda h, i: (0, 0, h)),
                  pl.BlockSpec((nc, HEAD_W, ATT_BLOCK), lambda h, i: (0, h, 0)),
                  pl.BlockSpec((4, ATT_DH), lambda h, i: (0, 0)),
                  pl.BlockSpec((1, HEAD_W), lambda h, i: (0, 0))],
        out_specs=pl.BlockSpec((ATT_QBLOCK, HEAD_W), lambda h, i: (i, h)),
        out_shape=jax.ShapeDtypeStruct((n, ATT_W), F32),
        scratch_shapes=[pltpu.VMEM((2, HEAD_W, ATT_QBLOCK), F32)]
                       + [pltpu.VMEM((2, ATT_BLOCK, ATT_QBLOCK), F32)] * 2,
        compiler_params=_params(("parallel", "arbitrary")),
        name="attn_prompt",
    )(q, kb, vtb, lam_p, g)


def _attn_sample_kernel(pt_ref, q_ref, kn_ref, vn_ref, *rest, lam_init, t_new):
    del pt_ref
    k_refs = rest[:PAGES_PER_STEP]
    v_refs = rest[PAGES_PER_STEP:2 * PAGES_PER_STEP]
    lam_ref, g_ref, o_ref, m_ref, l_ref, acc_ref = rest[2 * PAGES_PER_STEP:]
    step = pl.program_id(1)
    rows = ATT_HEADS * 2 * t_new
    page_cols = PAGE_SIZE * ATT_HEADS

    q = q_ref[...]
    first = lax.broadcasted_iota(jnp.int32, (t_new, HEAD_W), 1) < ATT_DH
    pieces = []
    for h in range(ATT_HEADS):
        qh = q[:, h * HEAD_W:(h + 1) * HEAD_W]
        pieces += [jnp.where(first, qh, 0.0), jnp.where(first, 0.0, qh)]
    qm = jnp.concatenate(pieces, axis=0).astype(BF16)

    def update(s, v):
        m_old = m_ref[...]
        m_new = jnp.maximum(m_old, jnp.max(s, axis=-1, keepdims=True))
        alpha = jnp.exp(m_old - m_new)
        p = jnp.exp(s - m_new)
        l_ref[...] = alpha * l_ref[...] + jnp.sum(p, axis=-1, keepdims=True)
        acc_ref[...] = alpha * acc_ref[...] + jnp.dot(p.astype(BF16), v, preferred_element_type=F32)
        m_ref[...] = m_new

    def logits(keys):
        return lax.dot_general(qm, keys, (((1,), (1,)), ((), ())), preferred_element_type=F32)

    @pl.when(step == 0)
    def _():
        m_ref[...] = jnp.full(m_ref.shape, NEG_INF, F32)
        l_ref[...] = jnp.zeros(l_ref.shape, F32)
        acc_ref[...] = jnp.zeros(acc_ref.shape, F32)
        zpad = jnp.zeros((PAGE_SIZE - ATT_HEADS * t_new, HEAD_W), F32)
        stack = lambda ref: jnp.concatenate(
            [ref[:, h * HEAD_W:(h + 1) * HEAD_W] for h in range(ATT_HEADS)] + [zpad], axis=0).astype(BF16)
        col = lax.broadcasted_iota(jnp.int32, (rows, PAGE_SIZE), 1)
        row = lax.broadcasted_iota(jnp.int32, (rows, PAGE_SIZE), 0)
        valid = (col // t_new == row // (2 * t_new)) & (col % t_new <= row % t_new)
        update(jnp.where(valid, logits(stack(kn_ref)), NEG_INF), stack(vn_ref))

    col = lax.broadcasted_iota(jnp.int32, (rows, page_cols), 1)
    row = lax.broadcasted_iota(jnp.int32, (rows, page_cols), 0)
    bias = jnp.where(col % ATT_HEADS == row // (2 * t_new), 0.0, NEG_INF)
    bias = jnp.concatenate([bias] * PAGES_PER_STEP, axis=1)
    keys = jnp.concatenate([r[...] for r in k_refs], axis=0).astype(BF16)
    vals = jnp.concatenate([r[...] for r in v_refs], axis=0).astype(BF16)
    update(logits(keys) + bias, vals)

    @pl.when(step == pl.num_programs(1) - 1)
    def _():
        lam = _lam(lam_ref, lam_init)
        acc = acc_ref[...]
        l = l_ref[...]
        outs = []
        for h in range(ATT_HEADS):
            r0, r1, r2 = 2 * h * t_new, (2 * h + 1) * t_new, (2 * h + 2) * t_new
            outs.append(_diff_combine(acc[r0:r1], l[r0:r1], acc[r1:r2], l[r1:r2], lam, g_ref[...], lam_init))
        o_ref[...] = jnp.concatenate(outs, axis=-1)


def _attn_sample(q, k_new, v_new, cache_k, cache_v, page_table, lam_p, g, *, layer, lam_init, t_new):
    n = q.shape[0]
    b = n // t_new
    n_pages = page_table.shape[1]
    steps = n_pages // PAGES_PER_STEP
    pool = cache_k.shape[1]
    ck = cache_k.reshape(DEPTH, pool, PAGE_SIZE * ATT_HEADS, HEAD_W)
    cv = cache_v.reshape(DEPTH, pool, PAGE_SIZE * ATT_HEADS, HEAD_W)
    row = pl.BlockSpec((t_new, ATT_W), lambda i, s, pt: (i, 0))

    def page_spec(j):
        return pl.BlockSpec((None, None, PAGE_SIZE * ATT_HEADS, HEAD_W),
                            lambda i, s, pt: (layer, pt[i, s * PAGES_PER_STEP + j], 0, 0))

    pages = [page_spec(j) for j in range(PAGES_PER_STEP)]
    rows = ATT_HEADS * 2 * t_new
    grid_spec = pltpu.PrefetchScalarGridSpec(
        num_scalar_prefetch=1,
        grid=(b, steps),
        in_specs=[row, row, row] + pages + pages + [
            pl.BlockSpec((4, ATT_DH), lambda i, s, pt: (0, 0)),
            pl.BlockSpec((1, HEAD_W), lambda i, s, pt: (0, 0))],
        out_specs=row,
        scratch_shapes=[pltpu.VMEM((rows, 1), F32), pltpu.VMEM((rows, 1), F32),
                        pltpu.VMEM((rows, HEAD_W), F32)],
    )
    return pl.pallas_call(
        functools.partial(_attn_sample_kernel, lam_init=lam_init, t_new=t_new),
        grid_spec=grid_spec,
        out_shape=jax.ShapeDtypeStruct((n, ATT_W), F32),
        compiler_params=_params(("parallel", "arbitrary")),
        name="attn_sample",
    )(page_table, q, k_new, v_new, *([ck] * PAGES_PER_STEP), *([cv] * PAGES_PER_STEP), lam_p, g)


def _rwkv_kernel(z_ref, sh_ref, s0_ref, mu_ref, w0_ref, w2_ref, a0_ref, a2_ref, g2_ref,
                 kk_ref, ka_ref, rk_ref, lng_ref, lnb_ref, seg_ref,
                 c_ref, shn_ref, sn_ref,
                 carry_ref, st_ref, r_s, w_s, k_s, kk_s, b_s, vt_s, yt_s, *, n_valid):
    tt = SCAN_TILE

    @pl.when(pl.program_id(1) == 0)
    def _():
        carry_ref[...] = sh_ref[0]
        st_ref[...] = s0_ref[0]

    z = z_ref[0]
    rowi = lax.broadcasted_iota(jnp.int32, z.shape, 0)
    prev = jnp.where(rowi == 0, carry_ref[...], pltpu.roll(z, 1, 0))
    shn_ref[0] = z[n_valid - 1:n_valid, :]
    carry_ref[...] = z[tt - 1:tt, :]
    zs = z + (prev - z) * mu_ref[...]

    def segsum(x):
        return jnp.dot(x, seg_ref[...], precision=HI, preferred_element_type=F32)

    r = zs[:, 0:RWKV_W]
    k = zs[:, RWKV_W:2 * RWKV_W]
    v = zs[:, 2 * RWKV_W:3 * RWKV_W]
    wa = zs[:, 3 * RWKV_W:3 * RWKV_W + W_LORA + A_LORA]
    gd = zs[:, 3 * RWKV_W + W_LORA + A_LORA:]
    wx = w0_ref[...] + jnp.dot(jnp.tanh(wa), w2_ref[...], precision=HI, preferred_element_type=F32)
    softplus = jnp.maximum(-wx, 0.0) + jnp.log1p(jnp.exp(-jnp.abs(wx)))
    w_log = -softplus - 0.5
    decay = jnp.exp(-jnp.exp(w_log))
    a = jax.nn.sigmoid(a0_ref[...] + jnp.dot(wa, a2_ref[...], precision=HI, preferred_element_type=F32))
    g = jnp.dot(jax.nn.sigmoid(gd), g2_ref[...], precision=HI, preferred_element_type=F32)
    kk = k * kk_ref[...]
    kk = kk * lax.rsqrt(segsum(kk * kk) + 1e-12)
    k = k * (1.0 + (a - 1.0) * ka_ref[...])
    bonus = segsum(r * k * rk_ref[...]) * v

    r_s[...] = r
    w_s[...] = decay
    k_s[...] = k
    kk_s[...] = kk
    b_s[...] = kk * a
    vt_s[...] = v.T
    yt_s[...] = jnp.zeros(yt_s.shape, F32)

    lane = lax.broadcasted_iota(jnp.int32, (RWKV_DH, 2 * RWKV_DH), 1)
    lo = lane < RWKV_DH

    def pair_sum(x):
        s_lo = jnp.sum(jnp.where(lo, x, 0.0), axis=-1, keepdims=True)
        s_hi = jnp.sum(jnp.where(lo, 0.0, x), axis=-1, keepdims=True)
        return s_lo, s_hi

    def step(t, rows8, i, states):
        hot = lane == t
        new_states = []
        for p in range(2):
            sl = slice(p * 2 * RWKV_DH, (p + 1) * 2 * RWKV_DH)
            row_t = lambda name: rows8[name][i:i + 1, sl]
            s_old = states[p]
            sa_lo, sa_hi = pair_sum(s_old * row_t("kk"))
            h0 = slice(2 * p * RWKV_DH, (2 * p + 1) * RWKV_DH)
            h1 = slice((2 * p + 1) * RWKV_DH, (2 * p + 2) * RWKV_DH)
            v_lo = jnp.sum(jnp.where(hot, vt_s[h0, :], 0.0), axis=-1, keepdims=True)
            v_hi = jnp.sum(jnp.where(hot, vt_s[h1, :], 0.0), axis=-1, keepdims=True)
            s_new = (s_old * row_t("w") - jnp.where(lo, sa_lo, sa_hi) * row_t("b")
                     + jnp.where(lo, v_lo, v_hi) * row_t("k"))
            y_lo, y_hi = pair_sum(s_new * row_t("r"))
            yt_s[h0, :] = jnp.where(hot, y_lo, yt_s[h0, :])
            yt_s[h1, :] = jnp.where(hot, y_hi, yt_s[h1, :])
            new_states.append(s_new)
        return tuple(new_states)

    sub = 8

    def group(gi, states):
        base = pl.multiple_of(gi * sub, sub)
        rows8 = {name: ref[pl.ds(base, sub), :]
                 for name, ref in (("r", r_s), ("w", w_s), ("k", k_s), ("kk", kk_s), ("b", b_s))}
        for i in range(sub):
            states = step(base + i, rows8, i, states)
        return states

    final = lax.fori_loop(0, n_valid // sub, group, (st_ref[0], st_ref[1]))
    st_ref[0] = final[0]
    st_ref[1] = final[1]
    sn_ref[0] = st_ref[...]

    y = yt_s[...].T
    mean = segsum(y) * (1.0 / RWKV_DH)
    d = y - mean
    var = segsum(d * d) * (1.0 / RWKV_DH)
    y = d * lax.rsqrt(var + RWKV_GN_EPS) * lng_ref[...] + lnb_ref[...]
    c_ref[0] = (y + bonus) * g


def _rwkv(z, shift, state, mu, w0, w2p, a0, a2p, g2, kk, ka, rk, lng, lnb, seg, *, n_valid):
    b, t, _ = z.shape
    tt = SCAN_TILE
    vec = lambda w: _const_spec((1, w))
    return pl.pallas_call(
        functools.partial(_rwkv_kernel, n_valid=n_valid),
        grid=(b, t // tt),
        in_specs=[pl.BlockSpec((1, tt, RWKV_IN), lambda i, j: (i, j, 0)),
                  pl.BlockSpec((1, 1, RWKV_IN), lambda i, j: (i, 0, 0)),
                  pl.BlockSpec((1, 2, RWKV_DH, 2 * RWKV_DH), lambda i, j: (i, 0, 0, 0)),
                  vec(RWKV_IN), vec(RWKV_W), _const_spec((W_LORA + A_LORA, RWKV_W)), vec(RWKV_W),
                  _const_spec((W_LORA + A_LORA, RWKV_W)), _const_spec((G_LORA, RWKV_W)),
                  vec(RWKV_W), vec(RWKV_W), vec(RWKV_W), vec(RWKV_W), vec(RWKV_W),
                  _const_spec((RWKV_W, RWKV_W))],
        out_specs=[pl.BlockSpec((1, tt, RWKV_W), lambda i, j: (i, j, 0)),
                   pl.BlockSpec((1, 1, RWKV_IN), lambda i, j: (i, 0, 0)),
                   pl.BlockSpec((1, 2, RWKV_DH, 2 * RWKV_DH), lambda i, j: (i, 0, 0, 0))],
        out_shape=[jax.ShapeDtypeStruct((b, t, RWKV_W), F32),
                   jax.ShapeDtypeStruct((b, 1, RWKV_IN), F32),
                   jax.ShapeDtypeStruct((b, 2, RWKV_DH, 2 * RWKV_DH), F32)],
        scratch_shapes=[pltpu.VMEM((1, RWKV_IN), F32), pltpu.VMEM((2, RWKV_DH, 2 * RWKV_DH), F32)]
                       + [pltpu.VMEM((tt, RWKV_W), F32)] * 5
                       + [pltpu.VMEM((RWKV_W, tt), F32)] * 2,
        compiler_params=_params(("parallel", "arbitrary")),
        name="rwkv7",
    )(z, shift, state, mu, w0, w2p, a0, a2p, g2, kk, ka, rk, lng, lnb, seg)


RW_CHUNK = 64
RW_TILE = 256
SUBLANES = 8
NN = (((1,), (0,)), ((), ()))
NT = (((1,), (1,)), ((), ()))


def _split(a):
    hi = a.astype(BF16)
    return hi, (a - hi.astype(F32)).astype(BF16)


def _dot3s(a, b, dims=NN):
    d = lambda x, y: lax.dot_general(x, y, dims, preferred_element_type=F32)
    return d(a[0], b[0]) + (d(a[0], b[1]) + d(a[1], b[0]))


def _dot3(a, b, dims=NN):
    return _dot3s(_split(a), _split(b), dims)


def _rwkv_chunk_kernel(z_ref, sh_ref, s0_ref, mu_ref, w0_ref, w2_ref, a0_ref, a2_ref, g2_ref,
                       kk_ref, ka_ref, rk_ref, lng_ref, lnb_ref, seg_ref,
                       c_ref, shn_ref, sn_ref, carry_ref, st_ref, y_s):
    tt, cs = RW_TILE, RW_CHUNK

    @pl.when(pl.program_id(1) == 0)
    def _():
        carry_ref[...] = sh_ref[0]
        st_ref[...] = s0_ref[0]

    z = z_ref[0]
    rowi = lax.broadcasted_iota(jnp.int32, z.shape, 0)
    prev = jnp.where(rowi == 0, carry_ref[...], pltpu.roll(z, 1, 0))
    shn_ref[0] = z[tt - 1:tt, :]
    carry_ref[...] = z[tt - 1:tt, :]
    zs = z + (prev - z) * mu_ref[...]

    seg = _split(seg_ref[...])

    def segsum(x):
        return _dot3s(_split(x), seg)

    r = zs[:, 0:RWKV_W]
    k = zs[:, RWKV_W:2 * RWKV_W]
    v = zs[:, 2 * RWKV_W:3 * RWKV_W]
    wa = zs[:, 3 * RWKV_W:3 * RWKV_W + W_LORA + A_LORA]
    gd = zs[:, 3 * RWKV_W + W_LORA + A_LORA:]
    wx = w0_ref[...] + _dot3(jnp.tanh(wa), w2_ref[...])
    softplus = jnp.maximum(-wx, 0.0) + jnp.log1p(jnp.exp(-jnp.abs(wx)))
    logw = -jnp.exp(-softplus - 0.5)
    a = jax.nn.sigmoid(a0_ref[...] + _dot3(wa, a2_ref[...]))
    g = _dot3(jax.nn.sigmoid(gd), g2_ref[...])
    kk = k * kk_ref[...]
    kk = kk * lax.rsqrt(segsum(kk * kk) + 1e-12)
    k = k * (1.0 + (a - 1.0) * ka_ref[...])
    bonus = segsum(r * k * rk_ref[...]) * v
    b = kk * a

    lane_head = lax.broadcasted_iota(jnp.int32, (1, RWKV_W), 1) // RWKV_DH
    head_mask = [lane_head == h for h in range(RWKV_HEADS)]
    row = lax.broadcasted_iota(jnp.int32, (2 * cs, cs), 0)
    col = lax.broadcasted_iota(jnp.int32, (2 * cs, cs), 1)
    tril2 = col <= jnp.where(row < cs, row - 1, row - cs)
    tri_incl = _split((lax.broadcasted_iota(jnp.int32, (cs, cs), 0)
                       >= lax.broadcasted_iota(jnp.int32, (cs, cs), 1)).astype(F32))
    bd_row = lax.broadcasted_iota(jnp.int32, (RWKV_W, RWKV_W), 0) // RWKV_DH
    bd_col = lax.broadcasted_iota(jnp.int32, (RWKV_W, RWKV_W), 1) // RWKV_DH
    block_diag = bd_row == bd_col
    lo_half = lax.broadcasted_iota(jnp.int32, (SUBLANES, 2 * RWKV_DH), 1) < RWKV_DH
    n_sub = cs // SUBLANES

    for ci in range(tt // cs):
        sl = slice(ci * cs, (ci + 1) * cs)
        r_c, lw_c, k_c, v_c, kk_c, b_c = r[sl], logw[sl], k[sl], v[sl], kk[sl], b[sl]
        cum = _dot3s(tri_incl, _split(lw_c))
        total = cum[cs - 1:cs, :]
        inv = jnp.exp(-cum)
        fin = jnp.exp(total - cum)
        lhs = _split(jnp.concatenate([kk_c * jnp.exp(cum - lw_c), r_c * jnp.exp(cum)], axis=0))
        kh, bh = _split(k_c * inv), _split(b_c * inv)
        v_s = _split(v_c)

        wy = jnp.zeros((2 * cs, RWKV_W), F32)
        l_mats, yb_mats = [], []
        for h in range(RWKV_HEADS):
            lhs_h = (jnp.where(head_mask[h], lhs[0], 0), jnp.where(head_mask[h], lhs[1], 0))
            gk = jnp.where(tril2, _dot3s(lhs_h, kh, NT), 0.0)
            gb = jnp.where(tril2, _dot3s(lhs_h, bh, NT), 0.0)
            wy = wy + jnp.where(head_mask[h], _dot3s(_split(gk), v_s), 0.0)
            l_mats.append(gb[:cs])
            yb_mats.append(gb[cs:])

        state = st_ref[...]
        sm = _dot3s(lhs, _split(state), NT)
        rhs = sm[:cs] + wy[:cs]
        tiles = [rhs[i * SUBLANES:(i + 1) * SUBLANES] for i in range(n_sub)]
        for j in range(cs):
            i0 = j // SUBLANES
            u_j = tiles[i0][j % SUBLANES:j % SUBLANES + 1, :]
            for i in range(i0, n_sub):
                rows = slice(i * SUBLANES, (i + 1) * SUBLANES)
                bc = lambda h: jnp.broadcast_to(l_mats[h][rows, j:j + 1], (SUBLANES, 2 * RWKV_DH))
                l_col = jnp.concatenate([jnp.where(lo_half, bc(0), bc(1)),
                                         jnp.where(lo_half, bc(2), bc(3))], axis=1)
                tiles[i] = tiles[i] - l_col * u_j
        u = jnp.concatenate(tiles, axis=0)
        u_s = _split(u)
        bu = jnp.zeros((cs, RWKV_W), F32)
        for h in range(RWKV_HEADS):
            bu = bu + jnp.where(head_mask[h], _dot3s(_split(yb_mats[h]), u_s), 0.0)
        y_s[sl, :] = sm[cs:] + wy[cs:] - bu
        delta = _dot3s(_split(v_c.T), _split(k_c * fin)) - _dot3s(_split(u.T), _split(b_c * fin))
        st_ref[...] = state * jnp.exp(total) + jnp.where(block_diag, delta, 0.0)

    sn_ref[0] = st_ref[...]
    y = y_s[...]
    mean = segsum(y) * (1.0 / RWKV_DH)
    d = y - mean
    var = segsum(d * d) * (1.0 / RWKV_DH)
    y = d * lax.rsqrt(var + RWKV_GN_EPS) * lng_ref[...] + lnb_ref[...]
    c_ref[0] = (y + bonus) * g


def _rwkv_chunked(z, shift, state, mu, w0, w2p, a0, a2p, g2, kk, ka, rk, lng, lnb, seg):
    b, t, _ = z.shape
    tt = RW_TILE
    vec = lambda w: _const_spec((1, w))
    return pl.pallas_call(
        _rwkv_chunk_kernel,
        grid=(b, t // tt),
        in_specs=[pl.BlockSpec((1, tt, RWKV_IN), lambda i, j: (i, j, 0)),
                  pl.BlockSpec((1, 1, RWKV_IN), lambda i, j: (i, 0, 0)),
                  pl.BlockSpec((1, RWKV_W, RWKV_W), lambda i, j: (i, 0, 0)),
                  vec(RWKV_IN), vec(RWKV_W), _const_spec((W_LORA + A_LORA, RWKV_W)), vec(RWKV_W),
                  _const_spec((W_LORA + A_LORA, RWKV_W)), _const_spec((G_LORA, RWKV_W)),
                  vec(RWKV_W), vec(RWKV_W), vec(RWKV_W), vec(RWKV_W), vec(RWKV_W),
                  _const_spec((RWKV_W, RWKV_W))],
        out_specs=[pl.BlockSpec((1, tt, RWKV_W), lambda i, j: (i, j, 0)),
                   pl.BlockSpec((1, 1, RWKV_IN), lambda i, j: (i, 0, 0)),
                   pl.BlockSpec((1, RWKV_W, RWKV_W), lambda i, j: (i, 0, 0))],
        out_shape=[jax.ShapeDtypeStruct((b, t, RWKV_W), F32),
                   jax.ShapeDtypeStruct((b, 1, RWKV_IN), F32),
                   jax.ShapeDtypeStruct((b, RWKV_W, RWKV_W), F32)],
        scratch_shapes=[pltpu.VMEM((1, RWKV_IN), F32), pltpu.VMEM((RWKV_W, RWKV_W), F32),
                        pltpu.VMEM((tt, RWKV_W), F32)],
        compiler_params=_params(("parallel", "arbitrary")),
        name="rwkv7_chunked",
    )(z, shift, state, mu, w0, w2p, a0, a2p, g2, kk, ka, rk, lng, lnb, seg)


def _block_diag_state(s):
    b = s.shape[0]
    eye = jnp.eye(RWKV_HEADS, dtype=s.dtype)
    return jnp.einsum("bhvk,hg->bhvgk", s, eye).reshape(b, RWKV_W, RWKV_W)


def _diag_blocks(s):
    b = s.shape[0]
    s = s.reshape(b, RWKV_HEADS, RWKV_DH, RWKV_HEADS, RWKV_DH)
    return jnp.stack([s[:, h, :, h, :] for h in range(RWKV_HEADS)], axis=1)


def _pair_state(s):
    b = s.shape[0]
    s = s.reshape(b, 2, 2, RWKV_DH, RWKV_DH)
    return jnp.transpose(s, (0, 1, 3, 2, 4)).reshape(b, 2, RWKV_DH, 2 * RWKV_DH)


def _unpair_state(s):
    b = s.shape[0]
    s = s.reshape(b, 2, RWKV_DH, 2, RWKV_DH)
    return jnp.transpose(s, (0, 1, 3, 2, 4)).reshape(b, RWKV_HEADS, RWKV_DH, RWKV_DH)


def _out_ffn_kernel(x_ref, a_ref, b_ref, c_ref, gs_ref, wo_ref, g2_ref, wg_ref, wu_ref, wd_ref,
                    fg_ref, o_ref, acc_ref, *, final):
    mix = jnp.concatenate([a_ref[...], b_ref[...], c_ref[...]], axis=-1) * gs_ref[...]
    x1 = x_ref[...] + jnp.dot(mix.astype(BF16), wo_ref[...], preferred_element_type=F32)
    acc_ref[...] = x1
    h2 = _rms(x1, g2_ref[...]).astype(BF16)
    for c in range(0, D_FF, FF_CHUNK):
        gate = jnp.dot(h2, wg_ref[:, c:c + FF_CHUNK], preferred_element_type=F32)
        up = jnp.dot(h2, wu_ref[:, c:c + FF_CHUNK], preferred_element_type=F32)
        act = (gate * jax.nn.sigmoid(gate) * up).astype(BF16)
        acc_ref[...] += jnp.dot(act, wd_ref[c:c + FF_CHUNK, :], preferred_element_type=F32)
    x2 = acc_ref[...]
    o_ref[...] = _rms(x2, fg_ref[...]) if final else x2


def _out_ffn(x, a, b, c, gs, wo, g2, wg, wu, wd, fg, *, tm, final):
    n = x.shape[0]
    row = lambda w: pl.BlockSpec((tm, w), lambda i: (i, 0))
    return pl.pallas_call(
        functools.partial(_out_ffn_kernel, final=final),
        grid=(n // tm,),
        in_specs=[row(D_MODEL), row(CONV_W), row(ATT_W), row(RWKV_W), _const_spec((1, D_MODEL)),
                  _const_spec((D_MODEL, D_MODEL)), _const_spec((1, D_MODEL)),
                  _const_spec((D_MODEL, D_FF)), _const_spec((D_MODEL, D_FF)),
                  _const_spec((D_FF, D_MODEL)), _const_spec((1, D_MODEL))],
        out_specs=row(D_MODEL),
        out_shape=jax.ShapeDtypeStruct((n, D_MODEL), F32),
        scratch_shapes=[pltpu.VMEM((tm, D_MODEL), F32)],
        compiler_params=_params(("parallel",)),
        name="out_ffn",
    )(x, a, b, c, gs, wo, g2, wg, wu, wd, fg)


def _rope_tables(pos):
    half = ATT_DH // 2
    inv = ROPE_THETA ** (-jnp.arange(half, dtype=F32) / half)
    ang = pos.astype(F32)[:, None] * inv[None, :]
    tile = lambda t: jnp.concatenate([t] * (HEAD_W // half), axis=-1)
    return tile(jnp.cos(ang)), tile(jnp.sin(ang))


def _rotate_half_columns(w):
    half = ATT_DH // 2
    w = w.reshape(w.shape[0], ATT_W // ATT_DH, 2, half)
    return jnp.concatenate([-w[:, :, 1], w[:, :, 0]], axis=-1).reshape(w.shape[0], ATT_W)


def kernel(x_prompt, x_sample, cache_k, cache_v, state_conv, state_shift, state_wkv, page_table, norm1_g, w_in, conv_w, conv_b, conv_ln_g, conv_ln_b, conv_pw_w, conv_pw_b, att_lambda, att_subln_g, rwkv_mu, rwkv_w0, rwkv_w2, rwkv_a0, rwkv_a2, rwkv_g2, rwkv_kk, rwkv_ka, rwkv_rk, rwkv_ln_g, rwkv_ln_b, group_scale, w_out, norm2_g, w_gate, w_up, w_down, final_g):
    bp, tp, _ = x_prompt.shape
    bs, ts, _ = x_sample.shape
    assert bp == 1 and tp % ATT_QBLOCK == 0 and ts <= SCAN_TILE and ts % 8 == 0
    past_len = page_table.shape[1] * PAGE_SIZE
    groups = {
        "p": dict(b=bp, t=tp, tm=512, conv_tt=256, x=x_prompt.reshape(bp * tp, D_MODEL),
                  pos=jnp.arange(tp, dtype=jnp.int32)),
        "s": dict(b=bs, t=ts, tm=bs * ts, conv_tt=ts, x=x_sample.reshape(bs * ts, D_MODEL),
                  pos=jnp.tile(past_len + jnp.arange(ts, dtype=jnp.int32), bs)),
    }
    for grp in groups.values():
        grp["cos"], grp["sin"] = _rope_tables(grp["pos"])
    outs = {name: dict(k=[], v=[], conv=[], shift=[], wkv=[]) for name in groups}

    seg = jnp.asarray(np.kron(np.eye(RWKV_HEADS, dtype=np.float32),
                              np.ones((RWKV_DH, RWKV_DH), np.float32)))
    row = lambda p: p.reshape(1, -1)

    for l in range(DEPTH):
        lam_init = 0.8 - 0.6 * math.exp(-0.3 * l)
        att_w = w_in[l][:, 2 * CONV_W:2 * CONV_W + 2 * ATT_W]
        w_ext = jnp.concatenate([w_in[l], _rotate_half_columns(att_w[:, :ATT_W]),
                                 _rotate_half_columns(att_w[:, ATT_W:])], axis=-1).astype(BF16)
        w2p = jnp.concatenate([rwkv_w2[l], jnp.zeros((A_LORA, RWKV_W), F32)], axis=0)
        a2p = jnp.concatenate([jnp.zeros((W_LORA, RWKV_W), F32), rwkv_a2[l]], axis=0)
        subln = row(att_subln_g[l])
        for name, grp in groups.items():
            b, t = grp["b"], grp["t"]
            prompt = name == "p"
            zc, q, k, v, zr, *kv_bf = _in_proj(grp["x"], row(norm1_g[l]), w_ext, grp["cos"], grp["sin"],
                                               tm=grp["tm"], emit_bf16=prompt)
            if prompt:
                conv_state = jnp.zeros((b, CONV_HALO, CONV_W), F32)
            else:
                conv_state = jnp.pad(state_conv[l], ((0, 0), (CONV_HALO - (CONV_K - 1), 0), (0, 0)))
            a_out, conv_new = _conv(zc.reshape(b, t, 2 * CONV_W), conv_state, conv_w[l].reshape(CONV_K, CONV_W),
                                    row(conv_b[l]), row(conv_ln_g[l]), row(conv_ln_b[l]),
                                    conv_pw_w[l].astype(BF16), row(conv_pw_b[l]), tt=grp["conv_tt"])
            if prompt:
                b_out = _attn_prompt(q, kv_bf[0], kv_bf[1], att_lambda[l], subln, lam_init=lam_init)
            else:
                b_out = _attn_sample(q, k, v, cache_k, cache_v, page_table, att_lambda[l], subln,
                                     layer=l, lam_init=lam_init, t_new=t)
            zr3 = zr.reshape(b, t, RWKV_IN)
            rw_params = (row(rwkv_mu[l]), row(rwkv_w0[l]), w2p, row(rwkv_a0[l]), a2p, rwkv_g2[l],
                         row(rwkv_kk[l]), row(rwkv_ka[l]), row(rwkv_rk[l]), row(rwkv_ln_g[l]),
                         row(rwkv_ln_b[l]), seg)
            if prompt:
                c_out, shift_new, wkv_new = _rwkv_chunked(
                    zr3, jnp.zeros((b, 1, RWKV_IN), F32), jnp.zeros((b, RWKV_W, RWKV_W), F32), *rw_params)
                wkv_new = _diag_blocks(wkv_new)
            else:
                zr3 = jnp.pad(zr3, ((0, 0), (0, SCAN_TILE - t), (0, 0)))
                c_out, shift_new, wkv_new = _rwkv(
                    zr3, state_shift[l].reshape(b, 1, RWKV_IN), _pair_state(state_wkv[l]), *rw_params,
                    n_valid=t)
                wkv_new = _unpair_state(wkv_new)
            c_out = c_out[:, :t].reshape(b * t, RWKV_W)
            grp["x"] = _out_ffn(grp["x"], a_out.reshape(b * t, CONV_W), b_out, c_out, row(group_scale[l]),
                                w_out[l].astype(BF16), row(norm2_g[l]), w_gate[l].astype(BF16),
                                w_up[l].astype(BF16), w_down[l].astype(BF16), row(final_g),
                                tm=grp["tm"], final=(l == DEPTH - 1))
            o = outs[name]
            o["k"].append(k.reshape(b, t, ATT_HEADS, HEAD_W))
            o["v"].append(v.reshape(b, t, ATT_HEADS, HEAD_W))
            o["conv"].append(conv_new[:, CONV_HALO - (CONV_K - 1):])
            o["shift"].append(shift_new.reshape(b, RWKV_IN))
            o["wkv"].append(wkv_new)

    y_prompt = groups["p"]["x"].reshape(bp, tp, D_MODEL)
    y_sample = groups["s"]["x"].reshape(bs, ts, D_MODEL)
    st = lambda name, key: jnp.stack(outs[name][key])
    return (y_prompt, y_sample,
            st("p", "k"), st("p", "v"), st("p", "conv"), st("p", "shift"), st("p", "wkv"),
            st("s", "k"), st("s", "v"), st("s", "conv"), st("s", "shift"), st("s", "wkv"))
```
